```python
import jax, jax.numpy as jnp
from jax import lax
import numpy as np

D_MODEL = 4096
BATCH = 4
SEQ = 2048
DEPTH = 1

NORM_EPS = 1e-6
N_MOD = 6
MIX_WIDTH = D_MODEL
GLA_WIDTH = MIX_WIDTH // 2
GLA_DV = 128
GLA_HEADS = GLA_WIDTH // GLA_DV
GLA_DK = GLA_DV // 2
GLA_KEY_WIDTH = GLA_HEADS * GLA_DK
GLA_GATE_RANK = 16
GLA_GATE_TEMP = 16.0
GLA_CHUNK = 64
SWA_HEAD_DIM = 64
SWA_Q_HEADS = (MIX_WIDTH - GLA_WIDTH) // SWA_HEAD_DIM
SWA_KV_HEADS = SWA_Q_HEADS // 8
SWA_Q_WIDTH = SWA_Q_HEADS * SWA_HEAD_DIM
SWA_KV_WIDTH = SWA_KV_HEADS * SWA_HEAD_DIM
SWA_WINDOW = 128
SWA_BLOCK = 128
ROPE_THETA = 10000.0
IN_SIZES = (GLA_KEY_WIDTH, GLA_KEY_WIDTH, GLA_WIDTH, GLA_WIDTH, GLA_GATE_RANK,
            SWA_Q_WIDTH, SWA_KV_WIDTH, SWA_KV_WIDTH)
IN_WIDTH = sum(IN_SIZES)
IN_OFFSETS = tuple(sum(IN_SIZES[:i + 1]) for i in range(len(IN_SIZES) - 1))
PEER_HEADS = 8
PEER_N_KEYS = 128
PEER_N_EXPERTS = PEER_N_KEYS * PEER_N_KEYS
PEER_QUERY_DIM = 256
PEER_HALF_DIM = PEER_QUERY_DIM // 2
PEER_TOPK = 16
PEER_TOKEN_BLOCK = 64

kernel_name = 'hymba_gla_swa_peer_adaln'


def _rms(x, gain=None):
    xf = x.astype(jnp.float32)
    y = xf * lax.rsqrt(jnp.mean(xf * xf, axis=-1, keepdims=True) + NORM_EPS)
    if gain is not None:
        y = y * gain.astype(jnp.float32)
    return y.astype(x.dtype)


def _modulate(x, shift, scale):
    return _rms(x) * (1 + scale) + shift


def _rope(x, pos):
    half = x.shape[-1] // 2
    inv = ROPE_THETA ** (-jnp.arange(half, dtype=jnp.float32) / half)
    ang = pos.astype(jnp.float32)[:, None] * inv[None, :]
    cos = jnp.cos(ang)[None, :, None, :]
    sin = jnp.sin(ang)[None, :, None, :]
    xf = x.astype(jnp.float32)
    x1, x2 = xf[..., :half], xf[..., half:]
    return jnp.concatenate([x1 * cos - x2 * sin, x2 * cos + x1 * sin], axis=-1).astype(x.dtype)


def _gla(q, k, v, log_a):
    B, S, H, dk = q.shape
    dv = v.shape[-1]
    C = GLA_CHUNK
    n = S // C

    def to_chunks(t):
        return t.reshape(B, n, C, H, t.shape[-1]).transpose(1, 0, 3, 2, 4)

    qc, kc, vc, ac = (to_chunks(t) for t in (q * (dk ** -0.5), k, v, log_a))
    causal = jnp.tril(jnp.ones((C, C), dtype=bool))

    def step(state, inp):
        qi, ki, vi, ai = inp
        b = jnp.cumsum(ai.astype(jnp.float32), axis=2)
        diff = b[:, :, :, None, :] - b[:, :, None, :, :]
        decay = jnp.exp(jnp.where(causal[:, :, None], diff, -jnp.inf))
        attn = jnp.einsum('bhtd,bhsd,bhtsd->bhts', qi, ki, decay)
        o = (jnp.einsum('bhts,bhsv->bhtv', attn, vi)
             + jnp.einsum('bhtd,bhdv->bhtv', qi * jnp.exp(b), state))
        b_last = b[:, :, -1:, :]
        state = (jnp.exp(b_last[:, :, 0, :])[..., None] * state
                 + jnp.einsum('bhsd,bhsv->bhdv', ki * jnp.exp(b_last - b), vi))
        return state, o

    state0 = jnp.zeros((B, H, dk, dv), jnp.float32)
    _, o = lax.scan(step, state0, (qc, kc, vc, ac))
    return o.transpose(1, 0, 3, 2, 4).reshape(B, S, H, dv)


def _swa(q, k, v, sinks):
    B, S, Hq, dh = q.shape
    Hkv = k.shape[2]
    G = Hq // Hkv
    W = SWA_BLOCK
    n = S // W
    qb = q.reshape(B, n, W, Hkv, G, dh)

    def with_prev(t):
        tb = t.reshape(B, n, W, Hkv, dh)
        prev = jnp.pad(tb[:, :-1], ((0, 0), (1, 0), (0, 0), (0, 0), (0, 0)))
        return jnp.concatenate([prev, tb], axis=2)

    kb, vb = with_prev(k), with_prev(v)
    s = jnp.einsum('bnqkgd,bnskd->bnkgqs', qb, kb).astype(jnp.float32) * (dh ** -0.5)
    qpos = jnp.arange(n)[:, None] * W + jnp.arange(W)[None, :]
    kpos = jnp.arange(n)[:, None] * W - W + jnp.arange(2 * W)[None, :]
    rel = qpos[:, :, None] - kpos[:, None, :]
    mask = (rel >= 0) & (rel < SWA_WINDOW) & (kpos[:, None, :] >= 0)
    s = jnp.where(mask[None, :, None, None], s, -jnp.inf)
    sink = sinks.astype(jnp.float32).reshape(Hkv, G)[None, None, :, :, None, None]
    m = jnp.maximum(jnp.max(s, axis=-1, keepdims=True), sink)
    p = jnp.exp(s - m)
    p = p / (jnp.sum(p, axis=-1, keepdims=True) + jnp.exp(sink - m))
    o = jnp.einsum('bnkgqs,bnskd->bnqkgd', p.astype(v.dtype), vb)
    return o.reshape(B, S, Hq, dh)


def _peer(y, w_q, keys_1, keys_2, down, up):
    B, S, D = y.shape
    H, K = PEER_HEADS, PEER_TOPK
    q = (y @ w_q).reshape(B, S, H, 2, PEER_HALF_DIM)
    s1 = jnp.einsum('bshd,kd->bshk', q[..., 0, :], keys_1).astype(jnp.float32)
    s2 = jnp.einsum('bshd,kd->bshk', q[..., 1, :], keys_2).astype(jnp.float32)
    v1, i1 = lax.top_k(s1, K)
    v2, i2 = lax.top_k(s2, K)
    cand = (v1[..., :, None] + v2[..., None, :]).reshape(B, S, H, K * K)
    cidx = (i1[..., :, None] * PEER_N_KEYS + i2[..., None, :]).reshape(B, S, H, K * K)
    top_s, pos = lax.top_k(cand, K)
    idx = jnp.take_along_axis(cidx, pos, axis=-1)
    gate = jax.nn.softmax(top_s, axis=-1)
    T = B * S
    nb = T // PEER_TOKEN_BLOCK
    yt = y.reshape(nb, PEER_TOKEN_BLOCK, D)
    it = idx.reshape(nb, PEER_TOKEN_BLOCK, H * K)
    gt = gate.reshape(nb, PEER_TOKEN_BLOCK, H * K)

    def block(args):
        yb, ib, gb = args
        u = jnp.take(down, ib, axis=0)
        vv = jnp.take(up, ib, axis=0)
        h = jax.nn.gelu(jnp.einsum('td,tkd->tk', yb, u).astype(jnp.float32), approximate=False) * gb
        return jnp.einsum('tk,tkd->td', h.astype(yb.dtype), vv)

    out = lax.map(block, (yt, it, gt))
    return out.reshape(B, S, D)


def setup_inputs(seed: int = 0) -> dict:
    key = jax.random.key(seed)
    ks = jax.random.split(key, 20)
    f32 = jnp.float32
    nrm = lambda k, shape, std: jax.random.normal(k, shape, f32) * std
    gain = lambda k, shape: 1.0 + 0.02 * jax.random.normal(k, shape, f32)
    L, D = DEPTH, D_MODEL
    return {
        'x': nrm(ks[0], (BATCH, SEQ, D), 1.0),
        'c': nrm(ks[1], (BATCH, D), 1.0),
        'w_ada': nrm(ks[2], (L, D, N_MOD * D), 0.005),
        'b_ada': nrm(ks[3], (L, N_MOD * D), 0.02),
        'w_in': nrm(ks[4], (L, D, IN_WIDTH), D ** -0.5),
        'w_gla_gate_up': nrm(ks[5], (L, GLA_GATE_RANK, GLA_KEY_WIDTH), GLA_GATE_RANK ** -0.5),
        'b_gla_gate': nrm(ks[6], (L, GLA_KEY_WIDTH), 0.1),
        'gla_out_norm': gain(ks[7], (L, GLA_DV)),
        'swa_q_norm': gain(ks[8], (L, SWA_HEAD_DIM)),
        'swa_k_norm': gain(ks[9], (L, SWA_HEAD_DIM)),
        'swa_sinks': nrm(ks[10], (L, SWA_Q_HEADS), 1.0),
        'swa_out_norm': gain(ks[11], (L, SWA_HEAD_DIM)),
        'w_out': nrm(ks[12], (L, MIX_WIDTH, D), MIX_WIDTH ** -0.5),
        'w_peer_q': nrm(ks[13], (L, D, PEER_HEADS * PEER_QUERY_DIM), D ** -0.5),
        'peer_sub_keys_1': nrm(ks[14], (L, PEER_N_KEYS, PEER_HALF_DIM), PEER_HALF_DIM ** -0.5),
        'peer_sub_keys_2': nrm(ks[15], (L, PEER_N_KEYS, PEER_HALF_DIM), PEER_HALF_DIM ** -0.5),
        'peer_expert_down': nrm(ks[16], (L, PEER_N_EXPERTS, D), D ** -0.5),
        'peer_expert_up': nrm(ks[17], (L, PEER_N_EXPERTS, D), 1.0),
    }


def reference(x, c, w_ada, b_ada, w_in, w_gla_gate_up, b_gla_gate, gla_out_norm,
              swa_q_norm, swa_k_norm, swa_sinks, swa_out_norm, w_out, w_peer_q,
              peer_sub_keys_1, peer_sub_keys_2, peer_expert_down, peer_expert_up):
    B, S, D = x.shape
    pos = jnp.arange(S)
    c_act = jax.nn.silu(c)
    for l in range(DEPTH):
        mod = (c_act @ w_ada[l] + b_ada[l])[:, None, :]
        sh1, sc1, g1, sh2, sc2, g2 = jnp.split(mod, N_MOD, axis=-1)

        h = _modulate(x, sh1, sc1)
        proj = h @ w_in[l]
        gq, gk, gv, gg, ga, sq, sk, sv = jnp.split(proj, IN_OFFSETS, axis=-1)

        log_a = jax.nn.log_sigmoid((ga @ w_gla_gate_up[l] + b_gla_gate[l]).astype(jnp.float32)) / GLA_GATE_TEMP
        o_gla = _gla(gq.reshape(B, S, GLA_HEADS, GLA_DK), gk.reshape(B, S, GLA_HEADS, GLA_DK),
                     gv.reshape(B, S, GLA_HEADS, GLA_DV), log_a.reshape(B, S, GLA_HEADS, GLA_DK))
        o_gla = _rms(o_gla, gla_out_norm[l]) * jax.nn.silu(gg.reshape(B, S, GLA_HEADS, GLA_DV))
        o_gla = o_gla.reshape(B, S, GLA_WIDTH)

        qh = _rope(_rms(sq.reshape(B, S, SWA_Q_HEADS, SWA_HEAD_DIM), swa_q_norm[l]), pos)
        kh = _rope(_rms(sk.reshape(B, S, SWA_KV_HEADS, SWA_HEAD_DIM), swa_k_norm[l]), pos)
        vh = sv.reshape(B, S, SWA_KV_HEADS, SWA_HEAD_DIM)
        o_swa = _rms(_swa(qh, kh, vh, swa_sinks[l]), swa_out_norm[l]).reshape(B, S, SWA_Q_WIDTH)

        mix = jnp.concatenate([o_gla.astype(x.dtype), o_swa.astype(x.dtype)], axis=-1) @ w_out[l]
        x = x + g1 * mix

        y = _modulate(x, sh2, sc2)
        x = x + g2 * _peer(y, w_peer_q[l], peer_sub_keys_1[l], peer_sub_keys_2[l],
                           peer_expert_down[l], peer_expert_up[l])
    return x
```

```python
import functools

import jax
import jax.numpy as jnp
from jax import lax
from jax.experimental import pallas as pl
from jax.experimental.pallas import tpu as pltpu

F32 = jnp.float32
BF16 = jnp.bfloat16

NORM_EPS = 1e-6
N_MOD = 6
GLA_GATE_RANK = 16
GLA_GATE_TEMP = 16.0
GLA_CHUNK = 64
SWA_BLOCK = 128
SWA_GROUP = 8
ROPE_THETA = 10000.0
PEER_TOPK = 16

LANES = 128
SUBLANES = 8
VMEM_LIMIT_BYTES = 60 * 1024 * 1024


def _params(*semantics):
    return pltpu.CompilerParams(dimension_semantics=semantics, vmem_limit_bytes=VMEM_LIMIT_BYTES)


def _dot(a, b):
    return jnp.dot(a, b, preferred_element_type=F32)


def _dot_nt(a, b):
    return lax.dot_general(a, b, (((1,), (1,)), ((), ())), preferred_element_type=F32)


def _dot_tn(a, b):
    return lax.dot_general(a, b, (((0,), (0,)), ((), ())), preferred_element_type=F32)


def _split(x, parts):
    out = []
    for _ in range(parts - 1):
        hi = x.astype(BF16)
        out.append(hi)
        x = x - hi.astype(F32)
    out.append(x.astype(BF16))
    return out


def _dot_split(x, m, parts):
    acc = None
    for p in _split(x, parts):
        t = _dot(p, m)
        acc = t if acc is None else acc + t
    return acc


def _modulated_rms(x, scale, shift):
    ms = jnp.mean(x * x, axis=-1, keepdims=True)
    return x * lax.rsqrt(ms + NORM_EPS) * (1.0 + scale) + shift


def _ada_kernel(c_ref, w_ref, b_ref, o_ref):
    c = c_ref[...]
    act = (c * jax.nn.sigmoid(c)).astype(BF16)
    o_ref[...] = _dot(act, w_ref[...].astype(BF16)) + b_ref[...]


def _ada(c_pad, w, b):
    rows, d = c_pad.shape
    n = w.shape[1]
    tn = 512
    return pl.pallas_call(
        _ada_kernel,
        grid=(n // tn,),
        in_specs=[pl.BlockSpec((rows, d), lambda j: (0, 0)),
                  pl.BlockSpec((d, tn), lambda j: (0, j)),
                  pl.BlockSpec((1, tn), lambda j: (0, j))],
        out_specs=pl.BlockSpec((rows, tn), lambda j: (0, j)),
        out_shape=jax.ShapeDtypeStruct((rows, n), F32),
        compiler_params=_params("arbitrary"),
        name="ada",
    )(c_pad, w, b)


def _inproj_kernel(x_ref, sh_ref, sc_ref, w_ref, o_ref, h_ref):
    @pl.when(pl.program_id(1) == 0)
    def _():
        h_ref[...] = _modulated_rms(x_ref[...], sc_ref[...], sh_ref[...]).astype(BF16)

    o_ref[...] = _dot(h_ref[...], w_ref[...])


def _inproj(x2, mod, w_p, seq, tm, tn):
    t, d = x2.shape
    n = w_p.shape[1]
    mod_spec = lambda k: pl.BlockSpec((None, None, 1, d), lambda i, j: (k, (i * tm) // seq, 0, 0))
    return pl.pallas_call(
        _inproj_kernel,
        grid=(t // tm, n // tn),
        in_specs=[pl.BlockSpec((tm, d), lambda i, j: (i, 0)),
                  mod_spec(0), mod_spec(1),
                  pl.BlockSpec((d, tn), lambda i, j: (0, j))],
        out_specs=pl.BlockSpec((tm, tn), lambda i, j: (i, j)),
        out_shape=jax.ShapeDtypeStruct((t, n), F32),
        scratch_shapes=[pltpu.VMEM((tm, d), BF16)],
        compiler_params=_params("arbitrary", "arbitrary"),
        name="inproj",
    )(x2, mod, mod, w_p)


def _gla_kernel(q_ref, k_ref, v_ref, g_ref, a_ref, wup_ref, bup_ref, gain_ref, o_ref, st_ref,
                *, n_pairs, n_chunks, dk, dv):
    @pl.when(pl.program_id(1) == 0)
    def _():
        st_ref[...] = jnp.zeros_like(st_ref)

    C = GLA_CHUNK
    row = lax.broadcasted_iota(jnp.int32, (C, C), 0)
    col = lax.broadcasted_iota(jnp.int32, (C, C), 1)
    tril = row >= col
    cum_op = tril.astype(BF16)
    causal2 = jnp.concatenate([tril, tril], axis=0)
    head0 = lax.broadcasted_iota(jnp.int32, (C, 2 * dk), 1) < dk
    head0_st = lax.broadcasted_iota(jnp.int32, (dv, 2 * dk), 1) < dk
    scale = dk ** -0.5
    gain = gain_ref[...]

    def stack_heads(x):
        return jnp.concatenate([jnp.where(head0, x, 0.0), jnp.where(head0, 0.0, x)], axis=0).astype(BF16)

    def chunk(c, carry):
        rows = pl.ds(pl.multiple_of(c * C, C), C)
        ga = a_ref[rows, :].astype(BF16)
        for p in range(n_pairs):
            lanes = slice(2 * dk * p, 2 * dk * (p + 1))
            z = _dot(ga, wup_ref[:, lanes]) + bup_ref[:, lanes]
            log_a = (jnp.minimum(z, 0.0) - jnp.log1p(jnp.exp(-jnp.abs(z)))) * (1.0 / GLA_GATE_TEMP)
            b = _dot_split_left(cum_op, log_a)
            b_last = b[C - 1:C, :]
            mid = 0.5 * b_last
            qt = q_ref[rows, lanes] * scale * jnp.exp(b - mid)
            kt = k_ref[rows, lanes] * jnp.exp(mid - b)
            qi = qt * jnp.exp(mid)
            kl = (kt * jnp.exp(b_last - mid)).astype(BF16)
            attn = _dot_nt(stack_heads(qt), kt.astype(BF16))
            attn = jnp.where(causal2, attn, 0.0).astype(BF16)
            st = st_ref[p]
            inter = _dot_nt(stack_heads(qi), st.astype(BF16))
            upd = []
            for hh in range(2):
                cols = slice(dv * (2 * p + hh), dv * (2 * p + hh + 1))
                vh = v_ref[rows, cols].astype(BF16)
                o = _dot(attn[C * hh:C * (hh + 1)], vh) + inter[C * hh:C * (hh + 1)]
                on = o * lax.rsqrt(jnp.mean(o * o, axis=-1, keepdims=True) + NORM_EPS) * gain
                gg = g_ref[rows, cols]
                o_ref[rows, cols] = (on * (gg * jax.nn.sigmoid(gg))).astype(BF16)
                upd.append(_dot_tn(vh, kl))
            st_ref[p] = st * jnp.exp(b_last) + jnp.where(head0_st, upd[0], upd[1])
        return carry

    lax.fori_loop(0, n_chunks, chunk, 0)


def _dot_split_left(m, x):
    acc = None
    for p in _split(x, 3):
        t = _dot(m, p)
        acc = t if acc is None else acc + t
    return acc


def _gla(proj, wup, bup, gain, batch, seq, ts, kw, gw, ga_col, dk, dv):
    t = proj.shape[0]
    nb = seq // ts
    rb = lambda b, s: b * nb + s
    kern = functools.partial(_gla_kernel, n_pairs=kw // (2 * dk), n_chunks=ts // GLA_CHUNK, dk=dk, dv=dv)
    return pl.pallas_call(
        kern,
        grid=(batch, nb),
        in_specs=[pl.BlockSpec((ts, kw), lambda b, s: (rb(b, s), 0)),
                  pl.BlockSpec((ts, kw), lambda b, s: (rb(b, s), 1)),
                  pl.BlockSpec((ts, gw), lambda b, s: (rb(b, s), (2 * kw) // gw)),
                  pl.BlockSpec((ts, gw), lambda b, s: (rb(b, s), (2 * kw) // gw + 1)),
                  pl.BlockSpec((ts, LANES), lambda b, s: (rb(b, s), ga_col // LANES)),
                  pl.BlockSpec(wup.shape, lambda b, s: (0, 0)),
                  pl.BlockSpec(bup.shape, lambda b, s: (0, 0)),
                  pl.BlockSpec(gain.shape, lambda b, s: (0, 0))],
        out_specs=pl.BlockSpec((ts, gw), lambda b, s: (rb(b, s), 0)),
        out_shape=jax.ShapeDtypeStruct((t, gw), BF16),
        scratch_shapes=[pltpu.VMEM((kw // (2 * dk), dv, 2 * dk), F32)],
        compiler_params=_params("arbitrary", "arbitrary"),
        name="gla",
    )(proj, proj, proj, proj, proj, wup, bup, gain)


def _group_rms(x, ones_bd, dh):
    ss = _dot_split(x * x, ones_bd, 2)
    return x * lax.rsqrt(ss * (1.0 / dh) + NORM_EPS)


def _norm_rope(x, gain, cos, sin_signed, ones_bd, swap_halves, dh):
    xn = _group_rms(x, ones_bd, dh) * gain
    return xn * cos + _dot_split(xn, swap_halves, 2) * sin_signed


def _swa_kernel(q_ref, kc_ref, kp_ref, vc_ref, vp_ref, cc_ref, cp_ref, sc_ref, sp_ref,
                gq_ref, gk_ref, go_ref, sink_ref, ones_ref, swap_ref, sel_ref, bdo_ref,
                o_ref, bd_ref, bdv_ref, p_ref, *, n_groups, dh):
    W, G = SWA_BLOCK, SWA_GROUP
    n = pl.program_id(1)

    @pl.when((pl.program_id(0) == 0) & (n == 0))
    def _():
        bd_ref[...] = jnp.zeros_like(bd_ref)

    gw = G * dh
    kvw = n_groups * dh
    k_all = jnp.concatenate([kp_ref[...], kc_ref[...]], axis=0)
    cos_k = jnp.concatenate([cp_ref[:, :kvw], cc_ref[:, :kvw]], axis=0)
    sin_k = jnp.concatenate([sp_ref[:, :kvw], sc_ref[:, :kvw]], axis=0)
    k_r = _norm_rope(k_all, gk_ref[...], cos_k, sin_k, ones_ref[:kvw, :kvw], swap_ref[:kvw, :kvw], dh)
    k_t = k_r.T.astype(BF16)
    v_all = jnp.concatenate([vp_ref[...], vc_ref[...]], axis=0).astype(BF16)

    qi = lax.broadcasted_iota(jnp.int32, (W, G * 2 * W), 0)
    kj = lax.broadcasted_iota(jnp.int32, (W, G * 2 * W), 1) & (2 * W - 1)
    rel = kj - qi - 1
    first_key = jnp.where(n > 0, 0, W)
    valid = (rel >= 0) & (rel < W) & (kj >= first_key)
    is_sink = kj == 0
    v_row = lax.broadcasted_iota(jnp.int32, (2 * W, gw), 0)
    v_lane = lax.broadcasted_iota(jnp.int32, (2 * W, gw), 1)

    for g in range(n_groups):
        q_r = _norm_rope(q_ref[:, gw * g:gw * (g + 1)], gq_ref[...], cc_ref[...], sc_ref[...],
                         ones_ref[...], swap_ref[...], dh).astype(BF16)
        k_tg = k_t[dh * g:dh * (g + 1), :]
        for h in range(G):
            bd_ref[dh * h:dh * (h + 1), 2 * W * h:2 * W * (h + 1)] = k_tg
        s = _dot(q_r, bd_ref[...]) * (dh ** -0.5)
        s = jnp.where(valid, s, jnp.where(is_sink, sink_ref[g], -jnp.inf))
        for h in range(G):
            seg = s[:, 2 * W * h:2 * W * (h + 1)]
            m = jnp.max(seg, axis=-1, keepdims=True)
            p_ref[:, 2 * W * h:2 * W * (h + 1)] = jnp.exp(seg - m).astype(BF16)
        v_rep = _dot(v_all, sel_ref[g]).astype(BF16)
        for h in range(G):
            keep = (v_lane >= dh * h) & (v_lane < dh * (h + 1)) & (v_row > 0)
            bdv_ref[2 * W * h:2 * W * (h + 1), :] = jnp.where(keep, v_rep, jnp.zeros_like(v_rep))
        p = p_ref[...]
        o = _dot(p, bdv_ref[...]) / _dot(p, bdo_ref[...])
        o_ref[:, gw * g:gw * (g + 1)] = (_group_rms(o, ones_ref[...], dh) * go_ref[...]).astype(BF16)


def _swa_constants(dh, n_groups, seq):
    G, W = SWA_GROUP, SWA_BLOCK
    gw = G * dh
    half = dh // 2
    i = jnp.arange(gw)
    ones_bd = (i[:, None] // dh == i[None, :] // dh).astype(BF16)
    partner = jnp.where(i % dh < half, i + half, i - half)
    swap = (i[:, None] == partner[None, :]).astype(BF16)
    kvw = n_groups * dh
    r = jnp.arange(kvw)
    sel = (r[None, :, None] == (dh * jnp.arange(n_groups)[:, None, None] + (i % dh)[None, None, :])).astype(BF16)
    bdones = (jnp.arange(G * 2 * W)[:, None] // (2 * W) == i[None, :] // dh).astype(BF16)
    inv = ROPE_THETA ** (-jnp.arange(half, dtype=F32) / half)
    ang = jnp.arange(seq, dtype=F32)[:, None] * inv[None, :]
    cos = jnp.tile(jnp.cos(ang), (1, 2 * G))
    sin = jnp.tile(jnp.concatenate([-jnp.sin(ang), jnp.sin(ang)], axis=1), (1, G))
    return ones_bd, swap, sel, bdones, cos, sin


def _swa(proj, q_gain, k_gain, o_gain, sinks, batch, seq, q_col, qw, kvw, dh):
    t = proj.shape[0]
    W, G = SWA_BLOCK, SWA_GROUP
    gw = G * dh
    n_groups = kvw // dh
    nb = seq // W
    ones_bd, swap, sel, bdones, cos, sin = _swa_constants(dh, n_groups, seq)
    gq = jnp.tile(q_gain, G)[None, :]
    gk = jnp.tile(k_gain, n_groups)[None, :]
    go = jnp.tile(o_gain, G)[None, :]
    sink_rows = jnp.repeat(sinks.reshape(n_groups, G), 2 * W, axis=1)[:, None, :]
    cur = lambda b, s: b * nb + s
    prev = lambda b, s: b * nb + jnp.maximum(s - 1, 0)
    k_blk = (q_col + qw) // kvw
    const = lambda a: pl.BlockSpec(a.shape, lambda b, s: (0,) * a.ndim)
    kern = functools.partial(_swa_kernel, n_groups=n_groups, dh=dh)
    return pl.pallas_call(
        kern,
        grid=(batch, nb),
        in_specs=[pl.BlockSpec((W, qw), lambda b, s: (cur(b, s), q_col // qw)),
                  pl.BlockSpec((W, kvw), lambda b, s: (cur(b, s), k_blk)),
                  pl.BlockSpec((W, kvw), lambda b, s: (prev(b, s), k_blk)),
                  pl.BlockSpec((W, kvw), lambda b, s: (cur(b, s), k_blk + 1)),
                  pl.BlockSpec((W, kvw), lambda b, s: (prev(b, s), k_blk + 1)),
                  pl.BlockSpec((W, gw), lambda b, s: (s, 0)),
                  pl.BlockSpec((W, gw), lambda b, s: (jnp.maximum(s - 1, 0), 0)),
                  pl.BlockSpec((W, gw), lambda b, s: (s, 0)),
                  pl.BlockSpec((W, gw), lambda b, s: (jnp.maximum(s - 1, 0), 0)),
                  const(gq), const(gk), const(go), const(sink_rows),
                  const(ones_bd), const(swap), const(sel), const(bdones)],
        out_specs=pl.BlockSpec((W, qw), lambda b, s: (cur(b, s), 0)),
        out_shape=jax.ShapeDtypeStruct((t, qw), BF16),
        scratch_shapes=[pltpu.VMEM((gw, G * 2 * W), BF16),
                        pltpu.VMEM((G * 2 * W, gw), BF16),
                        pltpu.VMEM((W, G * 2 * W), BF16)],
        compiler_params=_params("arbitrary", "arbitrary"),
        name="swa",
    )(proj, proj, proj, proj, proj, cos, cos, sin, sin, gq, gk, go, sink_rows, ones_bd, swap, sel, bdones)


def _outproj_kernel(og_ref, os_ref, wg_ref, ws_ref, x_ref, g_ref, o_ref):
    mix = _dot(og_ref[...], wg_ref[...]) + _dot(os_ref[...], ws_ref[...])
    o_ref[...] = x_ref[...] + g_ref[...] * mix


def _outproj(o_gla, o_swa, w_out, x2, mod, seq, tm, tn):
    t, d = x2.shape
    half = o_gla.shape[1]
    return pl.pallas_call(
        _outproj_kernel,
        grid=(t // tm, d // tn),
        in_specs=[pl.BlockSpec((tm, half), lambda i, j: (i, 0)),
                  pl.BlockSpec((tm, half), lambda i, j: (i, 0)),
                  pl.BlockSpec((half, tn), lambda i, j: (0, j)),
                  pl.BlockSpec((half, tn), lambda i, j: (1, j)),
                  pl.BlockSpec((tm, tn), lambda i, j: (i, j)),
                  pl.BlockSpec((None, None, 1, tn), lambda i, j: (2, (i * tm) // seq, 0, j))],
        out_specs=pl.BlockSpec((tm, tn), lambda i, j: (i, j)),
        out_shape=jax.ShapeDtypeStruct((t, d), F32),
        compiler_params=_params("arbitrary", "arbitrary"),
        name="outproj",
    )(o_gla, o_swa, w_out, w_out, x2, mod)


def _top_values(s, k):
    vals = []
    for _ in range(k):
        m = jnp.max(s, axis=0, keepdims=True)
        vals.append(m)
        s = jnp.where(s == m, -jnp.inf, s)
    return vals


def _peerq_kernel(x_ref, sh_ref, sc_ref, w_ref, k1_ref, k2_ref, yt_ref, s1_ref, s2_ref, tc_ref, y_ref,
                  *, heads_per_step, half):
    @pl.when(pl.program_id(1) == 0)
    def _():
        y = _modulated_rms(x_ref[...], sc_ref[...], sh_ref[...])
        y_ref[...] = y.astype(BF16)
        yt_ref[...] = y.T.astype(BF16)

    K = PEER_TOPK
    q = _dot(y_ref[...], w_ref[...])
    tm = q.shape[0]
    for hh in range(heads_per_step):
        q1 = q[:, 2 * half * hh:2 * half * hh + half].astype(BF16)
        q2 = q[:, 2 * half * hh + half:2 * half * (hh + 1)].astype(BF16)
        s1 = _dot_nt(k1_ref[...], q1)
        s2 = _dot_nt(k2_ref[...], q2)
        s1_ref[hh] = s1
        s2_ref[hh] = s2
        v1 = _top_values(s1, K)
        v2 = jnp.concatenate(_top_values(s2, K), axis=0)
        cand = [v1[0] + v2]
        for i in range(1, K // 2):
            cand.append(v1[i] + v2[:K // 2])
        cand.append(jnp.concatenate(v1[K // 2:], axis=0) + v2[0:1])
        top = _top_values(jnp.concatenate(cand, axis=0), K)
        m = top[0]
        z = jnp.exp(top[0] - m)
        for r in range(1, K):
            z = z + jnp.exp(top[r] - m)
        tc_ref[hh] = jnp.concatenate([top[K - 1], m + jnp.log(z), jnp.zeros((SUBLANES - 2, tm), F32)], axis=0)


def _peerq(x1, mod, w_q, k1, k2, seq, tm, n_heads):
    t, d = x1.shape
    nq = w_q.shape[1]
    n_keys, half = k1.shape
    hps = 2
    tn = hps * 2 * half
    mod_spec = lambda k: pl.BlockSpec((None, None, 1, d), lambda i, j: (k, (i * tm) // seq, 0, 0))
    kern = functools.partial(_peerq_kernel, heads_per_step=hps, half=half)
    return pl.pallas_call(
        kern,
        grid=(t // tm, nq // tn),
        in_specs=[pl.BlockSpec((tm, d), lambda i, j: (i, 0)),
                  mod_spec(3), mod_spec(4),
                  pl.BlockSpec((d, tn), lambda i, j: (0, j)),
                  pl.BlockSpec(k1.shape, lambda i, j: (0, 0)),
                  pl.BlockSpec(k2.shape, lambda i, j: (0, 0))],
        out_specs=[pl.BlockSpec((d, tm), lambda i, j: (0, i)),
                   pl.BlockSpec((hps, n_keys, tm), lambda i, j: (j, 0, i)),
                   pl.BlockSpec((hps, n_keys, tm), lambda i, j: (j, 0, i)),
                   pl.BlockSpec((hps, SUBLANES, tm), lambda i, j: (j, 0, i))],
        out_shape=[jax.ShapeDtypeStruct((d, t), BF16),
                   jax.ShapeDtypeStruct((n_heads, n_keys, t), F32),
                   jax.ShapeDtypeStruct((n_heads, n_keys, t), F32),
                   jax.ShapeDtypeStruct((n_heads, SUBLANES, t), F32)],
        scratch_shapes=[pltpu.VMEM((tm, d), BF16)],
        compiler_params=_params("arbitrary", "arbitrary"),
        name="peerq",
    )(x1, mod, mod, w_q, k1, k2)


def _peer_kernel(yt_ref, dn_ref, up_ref, s1_ref, s2_ref, tc_ref, x_ref, g_ref, o_ref, *, n_heads, n_keys, te):
    e = pl.program_id(1)

    @pl.when(e == 0)
    def _():
        o_ref[...] = jnp.zeros_like(o_ref)

    hd = _dot(dn_ref[...], yt_ref[...])
    per_step = te // n_keys
    pieces = []
    for ai in range(per_step):
        a = e * per_step + ai
        gate = None
        for h in range(n_heads):
            d = s1_ref[h, pl.ds(a, 1), :] + s2_ref[h]
            term = jnp.where(d >= tc_ref[h, 0:1, :], jnp.exp(d - tc_ref[h, 1:2, :]), 0.0)
            gate = term if gate is None else gate + term
        x = hd[n_keys * ai:n_keys * (ai + 1), :]
        act = 0.5 * x * (1.0 + lax.erf(x * (2.0 ** -0.5)))
        pieces.append((act * gate).astype(BF16))
    hs = jnp.concatenate(pieces, axis=0)
    o_ref[...] += _dot_tn(hs, up_ref[...])

    @pl.when(e == pl.num_programs(1) - 1)
    def _():
        o_ref[...] = x_ref[...] + g_ref[...] * o_ref[...]


def _peer(y_t, down, up, s1, s2, tc, x1, mod, seq, tm, te):
    d, t = y_t.shape
    n_exp = down.shape[0]
    n_heads, n_keys, _ = s1.shape
    once = dict(pipeline_mode=pl.Buffered(1))
    kern = functools.partial(_peer_kernel, n_heads=n_heads, n_keys=n_keys, te=te)
    return pl.pallas_call(
        kern,
        grid=(t // tm, n_exp // te),
        in_specs=[pl.BlockSpec((d, tm), lambda i, e: (0, i), **once),
                  pl.BlockSpec((te, d), lambda i, e: (e, 0)),
                  pl.BlockSpec((te, d), lambda i, e: (e, 0)),
                  pl.BlockSpec((n_heads, n_keys, tm), lambda i, e: (0, 0, i), **once),
                  pl.BlockSpec((n_heads, n_keys, tm), lambda i, e: (0, 0, i), **once),
                  pl.BlockSpec((n_heads, SUBLANES, tm), lambda i, e: (0, 0, i), **once),
                  pl.BlockSpec((tm, d), lambda i, e: (i, 0), **once),
                  pl.BlockSpec((None, None, 1, d), lambda i, e: (5, (i * tm) // seq, 0, 0))],
        out_specs=pl.BlockSpec((tm, d), lambda i, e: (i, 0)),
        out_shape=jax.ShapeDtypeStruct((t, d), F32),
        compiler_params=_params("arbitrary", "arbitrary"),
        name="peer",
    )(y_t, down, up, s1, s2, tc, x1, mod)


def _tile(total, want):
    return want if total % want == 0 else total


def kernel(x, c, w_ada, b_ada, w_in, w_gla_gate_up, b_gla_gate, gla_out_norm, swa_q_norm, swa_k_norm, swa_sinks, swa_out_norm, w_out, w_peer_q, peer_sub_keys_1, peer_sub_keys_2, peer_expert_down, peer_expert_up):
    batch, seq, d = x.shape
    t = batch * seq
    depth = w_ada.shape[0]
    kw = w_gla_gate_up.shape[2]
    dv = gla_out_norm.shape[1]
    dk = dv // 2
    gw = (kw // dk) * dv
    dh = swa_q_norm.shape[1]
    qw = swa_sinks.shape[1] * dh
    kvw = qw // SWA_GROUP
    in_width = w_in.shape[2]
    ga_old = 2 * kw + 2 * gw
    assert in_width == ga_old + GLA_GATE_RANK + qw + 2 * kvw
    assert gw == 2 * kw and qw == gw and d == gw + qw and 2 * dk == LANES and dv == LANES
    assert seq % SWA_BLOCK == 0 and kvw % LANES == 0
    n_keys, half = peer_sub_keys_1.shape[1:]
    n_heads = w_peer_q.shape[2] // (2 * half)
    assert n_keys == LANES and half == LANES and n_heads % 2 == 0

    q_col = ga_old
    ga_col = ga_old + qw + 2 * kvw
    n_proj = ga_col + LANES
    tn_in = 1280 if n_proj > 8 * 1280 // 2 else 640
    n_pad = -n_proj % tn_in

    tm_in = _tile(seq, 512)
    ts_gla = _tile(seq, 256)
    tm_out = _tile(seq, 1024)
    tm_peer = _tile(seq, 512)
    te = 512

    c_pad = jnp.pad(c, ((0, -batch % SUBLANES), (0, 0)))
    x2 = x.reshape(t, d)
    for l in range(depth):
        mod = _ada(c_pad, w_ada[l], b_ada[l][None, :])
        mod = mod[:batch].reshape(batch, N_MOD, d).transpose(1, 0, 2)[:, :, None, :]

        w = w_in[l]
        w_p = jnp.concatenate(
            [w[:, :ga_old], w[:, ga_old + GLA_GATE_RANK:], w[:, ga_old:ga_old + GLA_GATE_RANK],
             jnp.zeros((d, LANES - GLA_GATE_RANK + n_pad), w.dtype)], axis=1).astype(BF16)
        proj = _inproj(x2, mod, w_p, seq, tm_in, tn_in)

        wup = jnp.pad(w_gla_gate_up[l], ((0, LANES - GLA_GATE_RANK), (0, 0))).astype(BF16)
        o_gla = _gla(proj, wup, b_gla_gate[l][None, :], gla_out_norm[l][None, :],
                     batch, seq, ts_gla, kw, gw, ga_col, dk, dv)
        o_swa = _swa(proj, swa_q_norm[l], swa_k_norm[l], swa_out_norm[l], swa_sinks[l],
                     batch, seq, q_col, qw, kvw, dh)
        x1 = _outproj(o_gla, o_swa, w_out[l].astype(BF16), x2, mod, seq, tm_out, 512)

        y_t, s1, s2, tc = _peerq(x1, mod, w_peer_q[l].astype(BF16), peer_sub_keys_1[l].astype(BF16),
                                 peer_sub_keys_2[l].astype(BF16), seq, tm_peer, n_heads)
        x2 = _peer(y_t, peer_expert_down[l].astype(BF16), peer_expert_up[l].astype(BF16),
                   s1, s2, tc, x1, mod, seq, tm_peer, te)
    return x2.reshape(batch, seq, d)
```

```python
import functools

import jax
import jax.numpy as jnp
from jax import lax
from jax.experimental import pallas as pl
from jax.experimental.pallas import tpu as pltpu

F32 = jnp.float32
BF16 = jnp.bfloat16

NORM_EPS = 1e-6
N_MOD = 6
GLA_GATE_RANK = 16
GLA_GATE_TEMP = 16.0
GLA_CHUNK = 64
SWA_BLOCK = 128
SWA_GROUP = 8
ROPE_THETA = 10000.0
PEER_TOPK = 16

LANES = 128
SUBLANES = 8
MXU_TILE = 256
PEER_OUT_CHUNK = 512
VMEM_LIMIT_BYTES = 60 * 1024 * 1024


def _params(*semantics):
    return pltpu.CompilerParams(dimension_semantics=semantics, vmem_limit_bytes=VMEM_LIMIT_BYTES)


def _dot(a, b):
    return jnp.dot(a, b, preferred_element_type=F32)


def _dot_nt(a, b):
    return lax.dot_general(a, b, (((1,), (1,)), ((), ())), preferred_element_type=F32)


def _dot_tn(a, b):
    return lax.dot_general(a, b, (((0,), (0,)), ((), ())), preferred_element_type=F32)


def _split(x, parts):
    out = []
    for _ in range(parts - 1):
        hi = x.astype(BF16)
        out.append(hi)
        x = x - hi.astype(F32)
    out.append(x.astype(BF16))
    return out


def _dot_split(x, m, parts):
    acc = None
    for p in _split(x, parts):
        t = _dot(p, m)
        acc = t if acc is None else acc + t
    return acc


def _modulated_rms(x, scale, shift):
    ms = jnp.mean(x * x, axis=-1, keepdims=True)
    return x * lax.rsqrt(ms + NORM_EPS) * (1.0 + scale) + shift


def _ada_kernel(c_ref, w_ref, b_ref, o_ref):
    c = c_ref[...]
    act = (c * jax.nn.sigmoid(c)).astype(BF16)
    o_ref[...] = _dot(act, w_ref[...].astype(BF16)) + b_ref[...]


def _ada(c_pad, w, b):
    rows, d = c_pad.shape
    n = w.shape[1]
    tn = 512
    return pl.pallas_call(
        _ada_kernel,
        grid=(n // tn,),
        in_specs=[pl.BlockSpec((rows, d), lambda j: (0, 0)),
                  pl.BlockSpec((d, tn), lambda j: (0, j)),
                  pl.BlockSpec((1, tn), lambda j: (0, j))],
        out_specs=pl.BlockSpec((rows, tn), lambda j: (0, j)),
        out_shape=jax.ShapeDtypeStruct((rows, n), F32),
        compiler_params=_params("arbitrary"),
        name="ada",
    )(c_pad, w, b)


def _modulate_kernel(x_ref, sh_ref, sc_ref, o_ref):
    o_ref[...] = _modulated_rms(x_ref[...], sc_ref[...], sh_ref[...]).astype(BF16)


def _modulate(x2, mod, seq, tm):
    t, d = x2.shape
    mod_spec = lambda k: pl.BlockSpec((None, None, 1, d), lambda i: (k, (i * tm) // seq, 0, 0))
    return pl.pallas_call(
        _modulate_kernel,
        grid=(t // tm,),
        in_specs=[pl.BlockSpec((tm, d), lambda i: (i, 0)), mod_spec(0), mod_spec(1)],
        out_specs=pl.BlockSpec((tm, d), lambda i: (i, 0)),
        out_shape=jax.ShapeDtypeStruct((t, d), BF16),
        compiler_params=_params("arbitrary"),
        name="modulate",
    )(x2, mod, mod)


def _inproj_kernel(h_ref, wt_ref, o_ref, w_ref):
    @pl.when(pl.program_id(1) == 0)
    def _():
        w_ref[...] = wt_ref[...].T.astype(BF16)

    o_ref[...] = _dot(h_ref[...], w_ref[...])


def _inproj(h, w_t, tm, tn, ga_old, n_tail):
    t, d = h.shape
    n_main = ga_old // tn
    n_sw = n_tail // tn

    def src_row(j):
        s = SUBLANES
        return s * jnp.where(j < n_main, j * (tn // s),
                             jnp.where(j < n_main + n_sw, (ga_old + GLA_GATE_RANK) // s + (j - n_main) * (tn // s),
                                       ga_old // s))

    n_tiles = n_main + n_sw + 1
    return pl.pallas_call(
        _inproj_kernel,
        grid=(n_tiles, t // tm),
        in_specs=[pl.BlockSpec((tm, d), lambda j, i: (i, 0)),
                  pl.BlockSpec((pl.Element(tn), pl.Element(d)), lambda j, i: (src_row(j), 0))],
        out_specs=pl.BlockSpec((tm, tn), lambda j, i: (i, j)),
        out_shape=jax.ShapeDtypeStruct((t, n_tiles * tn), F32),
        scratch_shapes=[pltpu.VMEM((d, tn), BF16)],
        compiler_params=_params("arbitrary", "arbitrary"),
        name="inproj",
    )(h, w_t)


def _gla_kernel(q_ref, k_ref, v_ref, g_ref, a_ref, wup_ref, bup_ref, gain_ref, o_ref, st_ref,
                *, n_pairs, n_chunks, dk, dv):
    @pl.when(pl.program_id(1) == 0)
    def _():
        st_ref[...] = jnp.zeros_like(st_ref)

    C = GLA_CHUNK
    row = lax.broadcasted_iota(jnp.int32, (C, C), 0)
    col = lax.broadcasted_iota(jnp.int32, (C, C), 1)
    tril = row >= col
    cum_op = tril.astype(BF16)
    causal2 = jnp.concatenate([tril, tril], axis=0)
    head0 = lax.broadcasted_iota(jnp.int32, (C, 2 * dk), 1) < dk
    head0_st = lax.broadcasted_iota(jnp.int32, (dv, 2 * dk), 1) < dk
    scale = dk ** -0.5
    gain = gain_ref[...]

    def stack_heads(x):
        return jnp.concatenate([jnp.where(head0, x, 0.0), jnp.where(head0, 0.0, x)], axis=0).astype(BF16)

    def chunk(c, carry):
        rows = pl.ds(pl.multiple_of(c * C, C), C)
        ga = a_ref[rows, :].astype(BF16)
        for p in range(n_pairs):
            lanes = slice(2 * dk * p, 2 * dk * (p + 1))
            z = _dot(ga, wup_ref[:, lanes]) + bup_ref[:, lanes]
            log_a = (jnp.minimum(z, 0.0) - jnp.log1p(jnp.exp(-jnp.abs(z)))) * (1.0 / GLA_GATE_TEMP)
            b = _dot_split_left(cum_op, log_a)
            b_last = b[C - 1:C, :]
            mid = 0.5 * b_last
            qt = q_ref[rows, lanes] * scale * jnp.exp(b - mid)
            kt = k_ref[rows, lanes] * jnp.exp(mid - b)
            qi = qt * jnp.exp(mid)
            kl = (kt * jnp.exp(b_last - mid)).astype(BF16)
            attn = _dot_nt(stack_heads(qt), kt.astype(BF16))
            attn = jnp.where(causal2, attn, 0.0).astype(BF16)
            st = st_ref[p]
            inter = _dot_nt(stack_heads(qi), st.astype(BF16))
            upd = []
            for hh in range(2):
                cols = slice(dv * (2 * p + hh), dv * (2 * p + hh + 1))
                vh = v_ref[rows, cols].astype(BF16)
                o = _dot(attn[C * hh:C * (hh + 1)], vh) + inter[C * hh:C * (hh + 1)]
                on = o * lax.rsqrt(jnp.mean(o * o, axis=-1, keepdims=True) + NORM_EPS) * gain
                gg = g_ref[rows, cols]
                o_ref[rows, cols] = (on * (gg * jax.nn.sigmoid(gg))).astype(BF16)
                upd.append(_dot_tn(vh, kl))
            st_ref[p] = st * jnp.exp(b_last) + jnp.where(head0_st, upd[0], upd[1])
        return carry

    lax.fori_loop(0, n_chunks, chunk, 0)


def _dot_split_left(m, x):
    acc = None
    for p in _split(x, 3):
        t = _dot(m, p)
        acc = t if acc is None else acc + t
    return acc


def _gla(proj, wup, bup, gain, batch, seq, ts, kw, gw, ga_col, dk, dv):
    t = proj.shape[0]
    nb = seq // ts
    rb = lambda b, s: b * nb + s
    kern = functools.partial(_gla_kernel, n_pairs=kw // (2 * dk), n_chunks=ts // GLA_CHUNK, dk=dk, dv=dv)
    return pl.pallas_call(
        kern,
        grid=(batch, nb),
        in_specs=[pl.BlockSpec((ts, kw), lambda b, s: (rb(b, s), 0)),
                  pl.BlockSpec((ts, kw), lambda b, s: (rb(b, s), 1)),
                  pl.BlockSpec((ts, gw), lambda b, s: (rb(b, s), (2 * kw) // gw)),
                  pl.BlockSpec((ts, gw), lambda b, s: (rb(b, s), (2 * kw) // gw + 1)),
                  pl.BlockSpec((ts, LANES), lambda b, s: (rb(b, s), ga_col // LANES)),
                  pl.BlockSpec(wup.shape, lambda b, s: (0, 0)),
                  pl.BlockSpec(bup.shape, lambda b, s: (0, 0)),
                  pl.BlockSpec(gain.shape, lambda b, s: (0, 0))],
        out_specs=pl.BlockSpec((ts, gw), lambda b, s: (rb(b, s), 0)),
        out_shape=jax.ShapeDtypeStruct((t, gw), BF16),
        scratch_shapes=[pltpu.VMEM((kw // (2 * dk), dv, 2 * dk), F32)],
        compiler_params=_params("arbitrary", "arbitrary"),
        name="gla",
    )(proj, proj, proj, proj, proj, wup, bup, gain)


def _group_rms(x, ones_bd, dh):
    ss = _dot_split(x * x, ones_bd, 2)
    return x * lax.rsqrt(ss * (1.0 / dh) + NORM_EPS)


def _norm_rope(x, gain, cos, sin_signed, ones_bd, swap_halves, dh):
    xn = _group_rms(x, ones_bd, dh) * gain
    return xn * cos + _dot_split(xn, swap_halves, 2) * sin_signed


def _swa_kernel(q_ref, kc_ref, kp_ref, vc_ref, vp_ref, cc_ref, cp_ref, sc_ref, sp_ref,
                gq_ref, gk_ref, go_ref, sink_ref, ones_ref, swap_ref, sel_ref, bdo_ref,
                o_ref, bd_ref, bdv_ref, p_ref, *, n_groups, dh):
    W, G = SWA_BLOCK, SWA_GROUP
    n = pl.program_id(1)

    @pl.when((pl.program_id(0) == 0) & (n == 0))
    def _():
        bd_ref[...] = jnp.zeros_like(bd_ref)

    gw = G * dh
    kvw = n_groups * dh
    k_all = jnp.concatenate([kp_ref[...], kc_ref[...]], axis=0)
    cos_k = jnp.concatenate([cp_ref[:, :kvw], cc_ref[:, :kvw]], axis=0)
    sin_k = jnp.concatenate([sp_ref[:, :kvw], sc_ref[:, :kvw]], axis=0)
    k_r = _norm_rope(k_all, gk_ref[...], cos_k, sin_k, ones_ref[:kvw, :kvw], swap_ref[:kvw, :kvw], dh)
    k_t = k_r.T.astype(BF16)
    v_all = jnp.concatenate([vp_ref[...], vc_ref[...]], axis=0).astype(BF16)

    qi = lax.broadcasted_iota(jnp.int32, (W, G * 2 * W), 0)
    kj = lax.broadcasted_iota(jnp.int32, (W, G * 2 * W), 1) & (2 * W - 1)
    rel = kj - qi - 1
    first_key = jnp.where(n > 0, 0, W)
    valid = (rel >= 0) & (rel < W) & (kj >= first_key)
    is_sink = kj == 0
    v_row = lax.broadcasted_iota(jnp.int32, (2 * W, gw), 0)
    v_lane = lax.broadcasted_iota(jnp.int32, (2 * W, gw), 1)

    for g in range(n_groups):
        q_r = _norm_rope(q_ref[:, gw * g:gw * (g + 1)], gq_ref[...], cc_ref[...], sc_ref[...],
                         ones_ref[...], swap_ref[...], dh).astype(BF16)
        k_tg = k_t[dh * g:dh * (g + 1), :]
        for h in range(G):
            bd_ref[dh * h:dh * (h + 1), 2 * W * h:2 * W * (h + 1)] = k_tg
        s = _dot(q_r, bd_ref[...]) * (dh ** -0.5)
        s = jnp.where(valid, s, jnp.where(is_sink, sink_ref[g], -jnp.inf))
        for h in range(G):
            seg = s[:, 2 * W * h:2 * W * (h + 1)]
            m = jnp.max(seg, axis=-1, keepdims=True)
            p_ref[:, 2 * W * h:2 * W * (h + 1)] = jnp.exp(seg - m).astype(BF16)
        v_rep = _dot(v_all, sel_ref[g]).astype(BF16)
        for h in range(G):
            keep = (v_lane >= dh * h) & (v_lane < dh * (h + 1)) & (v_row > 0)
            bdv_ref[2 * W * h:2 * W * (h + 1), :] = jnp.where(keep, v_rep, jnp.zeros_like(v_rep))
        p = p_ref[...]
        o = _dot(p, bdv_ref[...]) / _dot(p, bdo_ref[...])
        o_ref[:, gw * g:gw * (g + 1)] = (_group_rms(o, ones_ref[...], dh) * go_ref[...]).astype(BF16)


def _swa_constants(dh, n_groups, seq):
    G, W = SWA_GROUP, SWA_BLOCK
    gw = G * dh
    half = dh // 2
    i = jnp.arange(gw)
    ones_bd = (i[:, None] // dh == i[None, :] // dh).astype(BF16)
    partner = jnp.where(i % dh < half, i + half, i - half)
    swap = (i[:, None] == partner[None, :]).astype(BF16)
    kvw = n_groups * dh
    r = jnp.arange(kvw)
    sel = (r[None, :, None] == (dh * jnp.arange(n_groups)[:, None, None] + (i % dh)[None, None, :])).astype(BF16)
    bdones = (jnp.arange(G * 2 * W)[:, None] // (2 * W) == i[None, :] // dh).astype(BF16)
    inv = ROPE_THETA ** (-jnp.arange(half, dtype=F32) / half)
    ang = jnp.arange(seq, dtype=F32)[:, None] * inv[None, :]
    cos = jnp.tile(jnp.cos(ang), (1, 2 * G))
    sin = jnp.tile(jnp.concatenate([-jnp.sin(ang), jnp.sin(ang)], axis=1), (1, G))
    return ones_bd, swap, sel, bdones, cos, sin


def _swa(proj, q_gain, k_gain, o_gain, sinks, batch, seq, q_col, qw, kvw, dh):
    t = proj.shape[0]
    W, G = SWA_BLOCK, SWA_GROUP
    gw = G * dh
    n_groups = kvw // dh
    nb = seq // W
    ones_bd, swap, sel, bdones, cos, sin = _swa_constants(dh, n_groups, seq)
    gq = jnp.tile(q_gain, G)[None, :]
    gk = jnp.tile(k_gain, n_groups)[None, :]
    go = jnp.tile(o_gain, G)[None, :]
    sink_rows = jnp.repeat(sinks.reshape(n_groups, G), 2 * W, axis=1)[:, None, :]
    cur = lambda b, s: b * nb + s
    prev = lambda b, s: b * nb + jnp.maximum(s - 1, 0)
    k_blk = (q_col + qw) // kvw
    const = lambda a: pl.BlockSpec(a.shape, lambda b, s: (0,) * a.ndim)
    kern = functools.partial(_swa_kernel, n_groups=n_groups, dh=dh)
    return pl.pallas_call(
        kern,
        grid=(batch, nb),
        in_specs=[pl.BlockSpec((W, qw), lambda b, s: (cur(b, s), q_col // qw)),
                  pl.BlockSpec((W, kvw), lambda b, s: (cur(b, s), k_blk)),
                  pl.BlockSpec((W, kvw), lambda b, s: (prev(b, s), k_blk)),
                  pl.BlockSpec((W, kvw), lambda b, s: (cur(b, s), k_blk + 1)),
                  pl.BlockSpec((W, kvw), lambda b, s: (prev(b, s), k_blk + 1)),
                  pl.BlockSpec((W, gw), lambda b, s: (s, 0)),
                  pl.BlockSpec((W, gw), lambda b, s: (jnp.maximum(s - 1, 0), 0)),
                  pl.BlockSpec((W, gw), lambda b, s: (s, 0)),
                  pl.BlockSpec((W, gw), lambda b, s: (jnp.maximum(s - 1, 0), 0)),
                  const(gq), const(gk), const(go), const(sink_rows),
                  const(ones_bd), const(swap), const(sel), const(bdones)],
        out_specs=pl.BlockSpec((W, qw), lambda b, s: (cur(b, s), 0)),
        out_shape=jax.ShapeDtypeStruct((t, qw), BF16),
        scratch_shapes=[pltpu.VMEM((gw, G * 2 * W), BF16),
                        pltpu.VMEM((G * 2 * W, gw), BF16),
                        pltpu.VMEM((W, G * 2 * W), BF16)],
        compiler_params=_params("arbitrary", "arbitrary"),
        name="swa",
    )(proj, proj, proj, proj, proj, cos, cos, sin, sin, gq, gk, go, sink_rows, ones_bd, swap, sel, bdones)


def _outproj_kernel(og_ref, os_ref, wg_ref, ws_ref, x_ref, g_ref, o_ref):
    mix = _dot(og_ref[...], wg_ref[...]) + _dot(os_ref[...], ws_ref[...])
    o_ref[...] = x_ref[...] + g_ref[...] * mix


def _outproj(o_gla, o_swa, w_out, x2, mod, seq, tm, tn):
    t, d = x2.shape
    half = o_gla.shape[1]
    return pl.pallas_call(
        _outproj_kernel,
        grid=(t // tm, d // tn),
        in_specs=[pl.BlockSpec((tm, half), lambda i, j: (i, 0)),
                  pl.BlockSpec((tm, half), lambda i, j: (i, 0)),
                  pl.BlockSpec((half, tn), lambda i, j: (0, j)),
                  pl.BlockSpec((half, tn), lambda i, j: (1, j)),
                  pl.BlockSpec((tm, tn), lambda i, j: (i, j)),
                  pl.BlockSpec((None, None, 1, tn), lambda i, j: (2, (i * tm) // seq, 0, j))],
        out_specs=pl.BlockSpec((tm, tn), lambda i, j: (i, j)),
        out_shape=jax.ShapeDtypeStruct((t, d), F32),
        compiler_params=_params("arbitrary", "arbitrary"),
        name="outproj",
    )(o_gla, o_swa, w_out, w_out, x2, mod)


def _top_values(s, k):
    vals = []
    for _ in range(k):
        m = jnp.max(s, axis=0, keepdims=True)
        vals.append(m)
        s = jnp.where(s == m, -jnp.inf, s)
    return vals


def _peerq_kernel(x_ref, sh_ref, sc_ref, w_ref, k1_ref, k2_ref, yt_ref, s1_ref, s2_ref, tc_ref, y_ref,
                  *, heads_per_step, half):
    @pl.when(pl.program_id(1) == 0)
    def _():
        y = _modulated_rms(x_ref[...], sc_ref[...], sh_ref[...])
        y_ref[...] = y.astype(BF16)
        yt_ref[...] = y.T.astype(BF16)

    K = PEER_TOPK
    q = _dot(y_ref[...], w_ref[...])
    tm = q.shape[0]
    for hh in range(heads_per_step):
        q1 = q[:, 2 * half * hh:2 * half * hh + half].astype(BF16)
        q2 = q[:, 2 * half * hh + half:2 * half * (hh + 1)].astype(BF16)
        s1 = _dot_nt(k1_ref[...], q1)
        s2 = _dot_nt(k2_ref[...], q2)
        s1_ref[hh] = s1
        s2_ref[hh] = s2
        v1 = _top_values(s1, K)
        v2 = jnp.concatenate(_top_values(s2, K), axis=0)
        cand = [v1[0] + v2]
        for i in range(1, K // 2):
            cand.append(v1[i] + v2[:K // 2])
        cand.append(jnp.concatenate(v1[K // 2:], axis=0) + v2[0:1])
        top = _top_values(jnp.concatenate(cand, axis=0), K)
        m = top[0]
        z = jnp.exp(top[0] - m)
        for r in range(1, K):
            z = z + jnp.exp(top[r] - m)
        tc_ref[hh] = jnp.concatenate([top[K - 1], m + jnp.log(z), jnp.zeros((SUBLANES - 2, tm), F32)], axis=0)


def _peerq(x1, mod, w_q, k1, k2, seq, tm, n_heads):
    t, d = x1.shape
    nq = w_q.shape[1]
    n_keys, half = k1.shape
    hps = 2
    tn = hps * 2 * half
    mod_spec = lambda k: pl.BlockSpec((None, None, 1, d), lambda i, j: (k, (i * tm) // seq, 0, 0))
    kern = functools.partial(_peerq_kernel, heads_per_step=hps, half=half)
    return pl.pallas_call(
        kern,
        grid=(t // tm, nq // tn),
        in_specs=[pl.BlockSpec((tm, d), lambda i, j: (i, 0)),
                  mod_spec(3), mod_spec(4),
                  pl.BlockSpec((d, tn), lambda i, j: (0, j)),
                  pl.BlockSpec(k1.shape, lambda i, j: (0, 0)),
                  pl.BlockSpec(k2.shape, lambda i, j: (0, 0))],
        out_specs=[pl.BlockSpec((d, tm), lambda i, j: (0, i)),
                   pl.BlockSpec((hps, n_keys, tm), lambda i, j: (j, 0, i)),
                   pl.BlockSpec((hps, n_keys, tm), lambda i, j: (j, 0, i)),
                   pl.BlockSpec((hps, SUBLANES, tm), lambda i, j: (j, 0, i))],
        out_shape=[jax.ShapeDtypeStruct((d, t), BF16),
                   jax.ShapeDtypeStruct((n_heads, n_keys, t), F32),
                   jax.ShapeDtypeStruct((n_heads, n_keys, t), F32),
                   jax.ShapeDtypeStruct((n_heads, SUBLANES, t), F32)],
        scratch_shapes=[pltpu.VMEM((tm, d), BF16)],
        compiler_params=_params("arbitrary", "arbitrary"),
        name="peerq",
    )(x1, mod, mod, w_q, k1, k2)


def _peer_kernel(yt_ref, dn_ref, up_ref, s1_ref, s2_ref, tc_ref, x_ref, g_ref, o_ref,
                 hd_ref, gate_ref, hst_ref, *, n_heads, n_keys, te):
    s = pl.program_id(1)
    n = pl.num_programs(1) - 1
    d, tm = yt_ref.shape

    @pl.when((pl.program_id(0) == 0) & (s == 0))
    def _():
        hst_ref[...] = jnp.zeros_like(hst_ref)

    @pl.when(s == 0)
    def _():
        o_ref[...] = jnp.zeros_like(o_ref)

    per_step = te // n_keys

    @pl.when(s < n)
    def _():
        n_kc = d // MXU_TILE
        per_a = n_kc // per_step
        tw = tm // per_a
        acc = None
        for kc in range(n_kc):
            ks = slice(MXU_TILE * kc, MXU_TILE * (kc + 1))
            part = _dot(dn_ref[:, ks], yt_ref[ks, :])
            acc = part if acc is None else acc + part
            ai, q = divmod(kc, per_a)
            a = s * per_step + ai
            cols = slice(tw * q, tw * (q + 1))
            gate = None
            for h in range(n_heads):
                cand = s1_ref[h, pl.ds(a, 1), :][:, cols] + s2_ref[h, :, cols]
                term = jnp.where(cand >= tc_ref[h, 0:1, cols], jnp.exp(cand - tc_ref[h, 1:2, cols]), 0.0)
                gate = term if gate is None else gate + term
            gate_ref[n_keys * ai:n_keys * (ai + 1), cols] = gate
        hd_ref[...] = acc

    cur = s % 2
    th = tm // 2
    pieces = [(ai, half) for ai in range(per_step) for half in range(2)]
    n_nc = d // PEER_OUT_CHUNK
    done = 0
    for j in range(n_nc):
        cols = slice(PEER_OUT_CHUNK * j, PEER_OUT_CHUNK * (j + 1))
        o_ref[:, cols] += _dot(hst_ref[1 - cur], up_ref[:, cols])
        upto = (j + 1) * len(pieces) // n_nc
        for ai, half in pieces[done:upto]:
            rows = slice(n_keys * ai, n_keys * (ai + 1))
            tok = slice(th * half, th * (half + 1))
            x = hd_ref[rows, tok]
            hs = 0.5 * x * (1.0 + lax.erf(x * (2.0 ** -0.5))) * gate_ref[rows, tok]
            hst_ref[cur, tok, rows] = hs.T.astype(BF16)
        done = upto

    @pl.when(s == n)
    def _():
        o_ref[...] = x_ref[...] + g_ref[...] * o_ref[...]
        hst_ref[1] = jnp.zeros_like(hst_ref[1])


def _peer(y_t, down, up, s1, s2, tc, x1, mod, seq, tm, te):
    d, t = y_t.shape
    n = down.shape[0] // te
    n_heads, n_keys, _ = s1.shape
    assert (d // MXU_TILE) % (te // n_keys) == 0 and d % PEER_OUT_CHUNK == 0
    once = dict(pipeline_mode=pl.Buffered(1))
    kern = functools.partial(_peer_kernel, n_heads=n_heads, n_keys=n_keys, te=te)
    return pl.pallas_call(
        kern,
        grid=(t // tm, n + 1),
        in_specs=[pl.BlockSpec((d, tm), lambda i, s: (0, i), **once),
                  pl.BlockSpec((te, d), lambda i, s: (jnp.minimum(s, n - 1), 0)),
                  pl.BlockSpec((te, d), lambda i, s: (jnp.maximum(s - 1, 0), 0)),
                  pl.BlockSpec((n_heads, n_keys, tm), lambda i, s: (0, 0, i), **once),
                  pl.BlockSpec((n_heads, n_keys, tm), lambda i, s: (0, 0, i), **once),
                  pl.BlockSpec((n_heads, SUBLANES, tm), lambda i, s: (0, 0, i), **once),
                  pl.BlockSpec((tm, d), lambda i, s: (i, 0), **once),
                  pl.BlockSpec((None, None, 1, d), lambda i, s: (5, (i * tm) // seq, 0, 0))],
        out_specs=pl.BlockSpec((tm, d), lambda i, s: (i, 0)),
        out_shape=jax.ShapeDtypeStruct((t, d), F32),
        scratch_shapes=[pltpu.VMEM((te, tm), F32), pltpu.VMEM((te, tm), F32), pltpu.VMEM((2, tm, te), BF16)],
        compiler_params=_params("arbitrary", "arbitrary"),
        name="peer",
    )(y_t, down, up, s1, s2, tc, x1, mod)


def _tile(total, want):
    return want if total % want == 0 else total


def kernel(x, c, w_ada, b_ada, w_in, w_gla_gate_up, b_gla_gate, gla_out_norm, swa_q_norm, swa_k_norm, swa_sinks, swa_out_norm, w_out, w_peer_q, peer_sub_keys_1, peer_sub_keys_2, peer_expert_down, peer_expert_up):
    batch, seq, d = x.shape
    t = batch * seq
    depth = w_ada.shape[0]
    kw = w_gla_gate_up.shape[2]
    dv = gla_out_norm.shape[1]
    dk = dv // 2
    gw = (kw // dk) * dv
    dh = swa_q_norm.shape[1]
    qw = swa_sinks.shape[1] * dh
    kvw = qw // SWA_GROUP
    in_width = w_in.shape[2]
    ga_old = 2 * kw + 2 * gw
    assert in_width == ga_old + GLA_GATE_RANK + qw + 2 * kvw
    assert gw == 2 * kw and qw == gw and d == gw + qw and 2 * dk == LANES and dv == LANES
    assert seq % SWA_BLOCK == 0 and kvw % LANES == 0
    n_keys, half = peer_sub_keys_1.shape[1:]
    n_heads = w_peer_q.shape[2] // (2 * half)
    assert n_keys == LANES and half == LANES and n_heads % 2 == 0

    q_col = ga_old
    n_tail = qw + 2 * kvw
    ga_col = ga_old + n_tail
    tn_in = 512 if ga_old % 512 == 0 and n_tail % 512 == 0 else 256
    assert ga_old % tn_in == 0 and n_tail % tn_in == 0 and qw >= tn_in

    tm_in = _tile(t, 1024)
    ts_gla = _tile(seq, 256)
    tm_out = _tile(seq, 1024)
    tm_peer = _tile(seq, 512)
    te = 512

    c_pad = jnp.pad(c, ((0, -batch % SUBLANES), (0, 0)))
    x2 = x.reshape(t, d)
    for l in range(depth):
        mod = _ada(c_pad, w_ada[l], b_ada[l][None, :])
        mod = mod[:batch].reshape(batch, N_MOD, d).transpose(1, 0, 2)[:, :, None, :]

        h = _modulate(x2, mod, seq, tm_peer)
        proj = _inproj(h, w_in[l].T, tm_in, tn_in, ga_old, n_tail)

        wup = jnp.pad(w_gla_gate_up[l], ((0, LANES - GLA_GATE_RANK), (0, 0))).astype(BF16)
        o_gla = _gla(proj, wup, b_gla_gate[l][None, :], gla_out_norm[l][None, :],
                     batch, seq, ts_gla, kw, gw, ga_col, dk, dv)
        o_swa = _swa(proj, swa_q_norm[l], swa_k_norm[l], swa_out_norm[l], swa_sinks[l],
                     batch, seq, q_col, qw, kvw, dh)
        x1 = _outproj(o_gla, o_swa, w_out[l].astype(BF16), x2, mod, seq, tm_out, 512)

        y_t, s1, s2, tc = _peerq(x1, mod, w_peer_q[l].astype(BF16), peer_sub_keys_1[l].astype(BF16),
                                 peer_sub_keys_2[l].astype(BF16), seq, tm_peer, n_heads)
        x2 = _peer(y_t, peer_expert_down[l].astype(BF16), peer_expert_up[l].astype(BF16),
                   s1, s2, tc, x1, mod, seq, tm_peer, te)
    return x2.reshape(batch, seq, d)
```

```python
import functools

import jax
import jax.numpy as jnp
from jax import lax
from jax.experimental import pallas as pl
from jax.experimental.pallas import tpu as pltpu

F32 = jnp.float32
BF16 = jnp.bfloat16

NORM_EPS = 1e-6
N_MOD = 6
GLA_GATE_RANK = 16
GLA_GATE_TEMP = 16.0
GLA_CHUNK = 64
SWA_BLOCK = 128
SWA_GROUP = 8
ROPE_THETA = 10000.0
PEER_TOPK = 16

LANES = 128
SUBLANES = 8
MXU_TILE = 256
PEER_OUT_CHUNK = 512
VMEM_LIMIT_BYTES = 60 * 1024 * 1024


def _params(*semantics):
    return pltpu.CompilerParams(dimension_semantics=semantics, vmem_limit_bytes=VMEM_LIMIT_BYTES)


def _dot(a, b):
    return jnp.dot(a, b, preferred_element_type=F32)


def _dot_nt(a, b):
    return lax.dot_general(a, b, (((1,), (1,)), ((), ())), preferred_element_type=F32)


def _dot_tn(a, b):
    return lax.dot_general(a, b, (((0,), (0,)), ((), ())), preferred_element_type=F32)


def _split(x, parts):
    out = []
    for _ in range(parts - 1):
        hi = x.astype(BF16)
        out.append(hi)
        x = x - hi.astype(F32)
    out.append(x.astype(BF16))
    return out


def _dot_split(x, m, parts):
    acc = None
    for p in _split(x, parts):
        t = _dot(p, m)
        acc = t if acc is None else acc + t
    return acc


def _modulated_rms(x, scale, shift):
    ms = jnp.mean(x * x, axis=-1, keepdims=True)
    return x * lax.rsqrt(ms + NORM_EPS) * (1.0 + scale) + shift


def _ada_kernel(c_ref, w_ref, b_ref, o_ref):
    c = c_ref[...]
    act = (c * jax.nn.sigmoid(c)).astype(BF16)
    o_ref[...] = _dot(act, w_ref[...].astype(BF16)) + b_ref[...]


def _ada(c_pad, w, b):
    rows, d = c_pad.shape
    n = w.shape[1]
    tn = 512
    return pl.pallas_call(
        _ada_kernel,
        grid=(n // tn,),
        in_specs=[pl.BlockSpec((rows, d), lambda j: (0, 0)),
                  pl.BlockSpec((d, tn), lambda j: (0, j)),
                  pl.BlockSpec((1, tn), lambda j: (0, j))],
        out_specs=pl.BlockSpec((rows, tn), lambda j: (0, j)),
        out_shape=jax.ShapeDtypeStruct((rows, n), F32),
        compiler_params=_params("arbitrary"),
        name="ada",
    )(c_pad, w, b)


def _modulate_kernel(x_ref, sh_ref, sc_ref, o_ref):
    o_ref[...] = _modulated_rms(x_ref[...], sc_ref[...], sh_ref[...]).astype(BF16)


def _modulate(x2, mod, seq, tm):
    t, d = x2.shape
    mod_spec = lambda k: pl.BlockSpec((None, None, 1, d), lambda i: (k, (i * tm) // seq, 0, 0))
    return pl.pallas_call(
        _modulate_kernel,
        grid=(t // tm,),
        in_specs=[pl.BlockSpec((tm, d), lambda i: (i, 0)), mod_spec(0), mod_spec(1)],
        out_specs=pl.BlockSpec((tm, d), lambda i: (i, 0)),
        out_shape=jax.ShapeDtypeStruct((t, d), BF16),
        compiler_params=_params("arbitrary"),
        name="modulate",
    )(x2, mod, mod)


def _inproj_kernel(h_ref, wt_ref, o_ref, w_ref):
    @pl.when(pl.program_id(1) == 0)
    def _():
        w_ref[...] = wt_ref[...].T.astype(BF16)

    o_ref[...] = _dot(h_ref[...], w_ref[...])


def _inproj(h, w_t, tm, tn, ga_old, n_tail):
    t, d = h.shape
    n_main = ga_old // tn
    n_sw = n_tail // tn

    def src_row(j):
        s = SUBLANES
        return s * jnp.where(j < n_main, j * (tn // s),
                             jnp.where(j < n_main + n_sw, (ga_old + GLA_GATE_RANK) // s + (j - n_main) * (tn // s),
                                       ga_old // s))

    n_tiles = n_main + n_sw + 1
    return pl.pallas_call(
        _inproj_kernel,
        grid=(n_tiles, t // tm),
        in_specs=[pl.BlockSpec((tm, d), lambda j, i: (i, 0)),
                  pl.BlockSpec((pl.Element(tn), pl.Element(d)), lambda j, i: (src_row(j), 0))],
        out_specs=pl.BlockSpec((tm, tn), lambda j, i: (i, j)),
        out_shape=jax.ShapeDtypeStruct((t, n_tiles * tn), F32),
        scratch_shapes=[pltpu.VMEM((d, tn), BF16)],
        compiler_params=_params("arbitrary", "arbitrary"),
        name="inproj",
    )(h, w_t)


def _gla_kernel(q_ref, k_ref, v_ref, g_ref, a_ref, wup_ref, bup_ref, gain_ref, o_ref, st_ref,
                *, n_pairs, n_chunks, dk, dv):
    @pl.when(pl.program_id(1) == 0)
    def _():
        st_ref[...] = jnp.zeros_like(st_ref)

    C = GLA_CHUNK
    row = lax.broadcasted_iota(jnp.int32, (C, C), 0)
    col = lax.broadcasted_iota(jnp.int32, (C, C), 1)
    tril = row >= col
    cum_op = tril.astype(BF16)
    causal2 = jnp.concatenate([tril, tril], axis=0)
    head0 = lax.broadcasted_iota(jnp.int32, (C, 2 * dk), 1) < dk
    head0_st = lax.broadcasted_iota(jnp.int32, (dv, 2 * dk), 1) < dk
    scale = dk ** -0.5
    gain = gain_ref[...]

    def stack_heads(x):
        return jnp.concatenate([jnp.where(head0, x, 0.0), jnp.where(head0, 0.0, x)], axis=0).astype(BF16)

    def chunk(c, carry):
        rows = pl.ds(pl.multiple_of(c * C, C), C)
        z = _dot(a_ref[rows, :].astype(BF16), wup_ref[...]) + bup_ref[...]
        log_a = (jnp.minimum(z, 0.0) - jnp.log1p(jnp.exp(-jnp.abs(z)))) * (1.0 / GLA_GATE_TEMP)
        b = _dot_split_left(cum_op, log_a)
        b_last = b[C - 1:C, :]
        mid = 0.5 * b_last
        qt = q_ref[rows, :] * scale * jnp.exp(b - mid)
        kt = k_ref[rows, :] * jnp.exp(mid - b)
        qi = qt * jnp.exp(mid)
        kl = (kt * jnp.exp(b_last - mid)).astype(BF16)
        kt = kt.astype(BF16)
        decay = jnp.exp(b_last)
        pair = lambda x, p: x[:, 2 * dk * p:2 * dk * (p + 1)]
        heads = range(2 * n_pairs)
        st = [st_ref[p] for p in range(n_pairs)]
        attn = [_dot_nt(stack_heads(pair(qt, p)), pair(kt, p)) for p in range(n_pairs)]
        inter = [_dot_nt(stack_heads(pair(qi, p)), st[p].astype(BF16)) for p in range(n_pairs)]
        attn = [jnp.where(causal2, a, 0.0).astype(BF16) for a in attn]
        v = [v_ref[rows, dv * h:dv * (h + 1)].astype(BF16) for h in heads]
        half = lambda x, h: x[h // 2][C * (h % 2):C * (h % 2 + 1)]
        out = [_dot(half(attn, h), v[h]) + half(inter, h) for h in heads]
        upd = [_dot_tn(v[h], pair(kl, h // 2)) for h in heads]
        for h in heads:
            o = out[h]
            on = o * lax.rsqrt(jnp.mean(o * o, axis=-1, keepdims=True) + NORM_EPS) * gain
            gg = g_ref[rows, dv * h:dv * (h + 1)]
            o_ref[rows, dv * h:dv * (h + 1)] = (on * (gg * jax.nn.sigmoid(gg))).astype(BF16)
        for p in range(n_pairs):
            st_ref[p] = st[p] * pair(decay, p) + jnp.where(head0_st, upd[2 * p], upd[2 * p + 1])
        return carry

    lax.fori_loop(0, n_chunks, chunk, 0)


def _dot_split_left(m, x):
    acc = None
    for p in _split(x, 3):
        t = _dot(m, p)
        acc = t if acc is None else acc + t
    return acc


def _gla(proj, wup, bup, gain, batch, seq, ts, kw, gw, ga_col, dk, dv):
    t = proj.shape[0]
    nb = seq // ts
    rb = lambda b, s: b * nb + s
    kern = functools.partial(_gla_kernel, n_pairs=kw // (2 * dk), n_chunks=ts // GLA_CHUNK, dk=dk, dv=dv)
    return pl.pallas_call(
        kern,
        grid=(batch, nb),
        in_specs=[pl.BlockSpec((ts, kw), lambda b, s: (rb(b, s), 0)),
                  pl.BlockSpec((ts, kw), lambda b, s: (rb(b, s), 1)),
                  pl.BlockSpec((ts, gw), lambda b, s: (rb(b, s), (2 * kw) // gw)),
                  pl.BlockSpec((ts, gw), lambda b, s: (rb(b, s), (2 * kw) // gw + 1)),
                  pl.BlockSpec((ts, LANES), lambda b, s: (rb(b, s), ga_col // LANES)),
                  pl.BlockSpec(wup.shape, lambda b, s: (0, 0)),
                  pl.BlockSpec(bup.shape, lambda b, s: (0, 0)),
                  pl.BlockSpec(gain.shape, lambda b, s: (0, 0))],
        out_specs=pl.BlockSpec((ts, gw), lambda b, s: (rb(b, s), 0)),
        out_shape=jax.ShapeDtypeStruct((t, gw), BF16),
        scratch_shapes=[pltpu.VMEM((kw // (2 * dk), dv, 2 * dk), F32)],
        compiler_params=_params("arbitrary", "arbitrary"),
        name="gla",
    )(proj, proj, proj, proj, proj, wup, bup, gain)


def _group_rms(x, ones_bd, dh):
    ss = _dot_split(x * x, ones_bd, 2)
    return x * lax.rsqrt(ss * (1.0 / dh) + NORM_EPS)


def _rope(x, cos, sin_signed, dh):
    half = dh // 2
    width = x.shape[1]
    first_half = (lax.broadcasted_iota(jnp.int32, x.shape, 1) & (dh - 1)) < half
    partner = jnp.where(first_half, pltpu.roll(x, width - half, 1), pltpu.roll(x, half, 1))
    return x * cos + partner * sin_signed


def _swa_kernel(q_ref, kc_ref, kp_ref, vc_ref, vp_ref, cc_ref, cp_ref, sc_ref, sp_ref,
                gq_ref, gk_ref, go_ref, sink_ref, ones_ref, sel_ref, o_ref, *, n_groups, dh):
    W, G = SWA_BLOCK, SWA_GROUP
    n = pl.program_id(1)
    gw = G * dh
    kvw = n_groups * dh
    pairs = G // 2

    k_all = jnp.concatenate([kp_ref[...], kc_ref[...]], axis=0)
    cos_k = jnp.concatenate([cp_ref[:, :kvw], cc_ref[:, :kvw]], axis=0)
    sin_k = jnp.concatenate([sp_ref[:, :kvw], sc_ref[:, :kvw]], axis=0)
    k_r = _rope(_group_rms(k_all, ones_ref[:kvw, :kvw], dh) * gk_ref[...], cos_k, sin_k, dh)
    k_t = k_r.T.astype(BF16)
    v_all = jnp.concatenate([vp_ref[...], vc_ref[...]], axis=0).astype(BF16)

    qi = lax.broadcasted_iota(jnp.int32, (2 * W, 2 * W), 0) & (W - 1)
    kj = lax.broadcasted_iota(jnp.int32, (2 * W, 2 * W), 1)
    rel = kj - qi - 1
    first_key = jnp.where(n > 0, 0, W)
    valid = (rel >= 0) & (rel < W) & (kj >= first_key)
    head0 = lax.broadcasted_iota(jnp.int32, (W, 2 * dh), 1) < dh
    ones_cols = jnp.ones((2 * W, 2 * dh), BF16)

    for g in range(n_groups):
        q_n = _group_rms(q_ref[:, gw * g:gw * (g + 1)], ones_ref[...], dh) * gq_ref[...]
        q_r = _rope(q_n, cc_ref[...], sc_ref[...], dh)
        k_tg = k_t[dh * g:dh * (g + 1), :]
        k_t2 = jnp.concatenate([k_tg, k_tg], axis=0)
        v3 = jnp.concatenate([_dot(v_all, sel_ref[g]).astype(BF16), ones_cols], axis=1)
        outs = []
        for pp in range(pairs):
            qp = q_r[:, 2 * dh * pp:2 * dh * (pp + 1)]
            q2 = jnp.concatenate([jnp.where(head0, qp, 0.0), jnp.where(head0, 0.0, qp)], axis=0).astype(BF16)
            s = jnp.where(valid, _dot(q2, k_t2) * (dh ** -0.5), -jnp.inf)
            sink = sink_ref[pairs * g + pp]
            m = jnp.maximum(jnp.max(s, axis=-1, keepdims=True), sink)
            r = _dot(jnp.exp(s - m).astype(BF16), v3)
            o2 = r[:, :2 * dh] / (r[:, 2 * dh:] + jnp.exp(sink - m))
            outs.append(jnp.where(head0, o2[:W], o2[W:]))
        o = jnp.concatenate(outs, axis=1)
        o_ref[:, gw * g:gw * (g + 1)] = (_group_rms(o, ones_ref[...], dh) * go_ref[...]).astype(BF16)


def _swa_constants(dh, n_groups, seq):
    G = SWA_GROUP
    gw = G * dh
    half = dh // 2
    i = jnp.arange(gw)
    ones_bd = (i[:, None] // dh == i[None, :] // dh).astype(BF16)
    kvw = n_groups * dh
    j = jnp.arange(2 * dh)
    sel = (jnp.arange(kvw)[None, :, None] == (dh * jnp.arange(n_groups)[:, None, None] + (j % dh)[None, None, :])).astype(BF16)
    inv = ROPE_THETA ** (-jnp.arange(half, dtype=F32) / half)
    ang = jnp.arange(seq, dtype=F32)[:, None] * inv[None, :]
    cos = jnp.tile(jnp.cos(ang), (1, 2 * G))
    sin = jnp.tile(jnp.concatenate([-jnp.sin(ang), jnp.sin(ang)], axis=1), (1, G))
    return ones_bd, sel, cos, sin


def _swa(proj, q_gain, k_gain, o_gain, sinks, batch, seq, q_col, qw, kvw, dh):
    t = proj.shape[0]
    W, G = SWA_BLOCK, SWA_GROUP
    gw = G * dh
    n_groups = kvw // dh
    nb = seq // W
    assert dh & (dh - 1) == 0 and 2 * dh == LANES
    ones_bd, sel, cos, sin = _swa_constants(dh, n_groups, seq)
    gq = jnp.tile(q_gain, G)[None, :]
    gk = jnp.tile(k_gain, n_groups)[None, :]
    go = jnp.tile(o_gain, G)[None, :]
    sink_cols = jnp.repeat(sinks.reshape(-1, 2), W, axis=1)[:, :, None]
    cur = lambda b, s: b * nb + s
    prev = lambda b, s: b * nb + jnp.maximum(s - 1, 0)
    k_blk = (q_col + qw) // kvw
    const = lambda a: pl.BlockSpec(a.shape, lambda b, s: (0,) * a.ndim)
    kern = functools.partial(_swa_kernel, n_groups=n_groups, dh=dh)
    return pl.pallas_call(
        kern,
        grid=(batch, nb),
        in_specs=[pl.BlockSpec((W, qw), lambda b, s: (cur(b, s), q_col // qw)),
                  pl.BlockSpec((W, kvw), lambda b, s: (cur(b, s), k_blk)),
                  pl.BlockSpec((W, kvw), lambda b, s: (prev(b, s), k_blk)),
                  pl.BlockSpec((W, kvw), lambda b, s: (cur(b, s), k_blk + 1)),
                  pl.BlockSpec((W, kvw), lambda b, s: (prev(b, s), k_blk + 1)),
                  pl.BlockSpec((W, gw), lambda b, s: (s, 0)),
                  pl.BlockSpec((W, gw), lambda b, s: (jnp.maximum(s - 1, 0), 0)),
                  pl.BlockSpec((W, gw), lambda b, s: (s, 0)),
                  pl.BlockSpec((W, gw), lambda b, s: (jnp.maximum(s - 1, 0), 0)),
                  const(gq), const(gk), const(go), const(sink_cols), const(ones_bd), const(sel)],
        out_specs=pl.BlockSpec((W, qw), lambda b, s: (cur(b, s), 0)),
        out_shape=jax.ShapeDtypeStruct((t, qw), BF16),
        compiler_params=_params("arbitrary", "arbitrary"),
        name="swa",
    )(proj, proj, proj, proj, proj, cos, cos, sin, sin, gq, gk, go, sink_cols, ones_bd, sel)


def _outproj_kernel(og_ref, os_ref, wg_ref, ws_ref, x_ref, g_ref, o_ref):
    mix = _dot(og_ref[...], wg_ref[...]) + _dot(os_ref[...], ws_ref[...])
    o_ref[...] = x_ref[...] + g_ref[...] * mix


def _outproj(o_gla, o_swa, w_out, x2, mod, seq, tm, tn):
    t, d = x2.shape
    half = o_gla.shape[1]
    return pl.pallas_call(
        _outproj_kernel,
        grid=(t // tm, d // tn),
        in_specs=[pl.BlockSpec((tm, half), lambda i, j: (i, 0)),
                  pl.BlockSpec((tm, half), lambda i, j: (i, 0)),
                  pl.BlockSpec((half, tn), lambda i, j: (0, j)),
                  pl.BlockSpec((half, tn), lambda i, j: (1, j)),
                  pl.BlockSpec((tm, tn), lambda i, j: (i, j)),
                  pl.BlockSpec((None, None, 1, tn), lambda i, j: (2, (i * tm) // seq, 0, j))],
        out_specs=pl.BlockSpec((tm, tn), lambda i, j: (i, j)),
        out_shape=jax.ShapeDtypeStruct((t, d), F32),
        compiler_params=_params("arbitrary", "arbitrary"),
        name="outproj",
    )(o_gla, o_swa, w_out, w_out, x2, mod)


def _top_values(s, k):
    vals = []
    for _ in range(k):
        m = jnp.max(s, axis=0, keepdims=True)
        vals.append(m)
        s = jnp.where(s == m, -jnp.inf, s)
    return vals


def _peerq_kernel(x_ref, sh_ref, sc_ref, w_ref, k1_ref, k2_ref, yt_ref, s1_ref, s2_ref, tc_ref, y_ref,
                  *, heads_per_step, half):
    @pl.when(pl.program_id(1) == 0)
    def _():
        y = _modulated_rms(x_ref[...], sc_ref[...], sh_ref[...])
        y_ref[...] = y.astype(BF16)
        yt_ref[...] = y.T.astype(BF16)

    K = PEER_TOPK
    q = _dot(y_ref[...], w_ref[...])
    tm = q.shape[0]
    for hh in range(heads_per_step):
        q1 = q[:, 2 * half * hh:2 * half * hh + half].astype(BF16)
        q2 = q[:, 2 * half * hh + half:2 * half * (hh + 1)].astype(BF16)
        s1 = _dot_nt(k1_ref[...], q1)
        s2 = _dot_nt(k2_ref[...], q2)
        s1_ref[hh] = s1
        s2_ref[hh] = s2
        v1 = _top_values(s1, K)
        v2 = jnp.concatenate(_top_values(s2, K), axis=0)
        cand = [v1[0] + v2]
        for i in range(1, K // 2):
            cand.append(v1[i] + v2[:K // 2])
        cand.append(jnp.concatenate(v1[K // 2:], axis=0) + v2[0:1])
        top = _top_values(jnp.concatenate(cand, axis=0), K)
        m = top[0]
        z = jnp.exp(top[0] - m)
        for r in range(1, K):
            z = z + jnp.exp(top[r] - m)
        tc_ref[hh] = jnp.concatenate([top[K - 1], m + jnp.log(z), jnp.zeros((SUBLANES - 2, tm), F32)], axis=0)


def _peerq(x1, mod, w_q, k1, k2, seq, tm, n_heads):
    t, d = x1.shape
    nq = w_q.shape[1]
    n_keys, half = k1.shape
    hps = 2
    tn = hps * 2 * half
    mod_spec = lambda k: pl.BlockSpec((None, None, 1, d), lambda i, j: (k, (i * tm) // seq, 0, 0))
    kern = functools.partial(_peerq_kernel, heads_per_step=hps, half=half)
    return pl.pallas_call(
        kern,
        grid=(t // tm, nq // tn),
        in_specs=[pl.BlockSpec((tm, d), lambda i, j: (i, 0)),
                  mod_spec(3), mod_spec(4),
                  pl.BlockSpec((d, tn), lambda i, j: (0, j)),
                  pl.BlockSpec(k1.shape, lambda i, j: (0, 0)),
                  pl.BlockSpec(k2.shape, lambda i, j: (0, 0))],
        out_specs=[pl.BlockSpec((d, tm), lambda i, j: (0, i)),
                   pl.BlockSpec((hps, n_keys, tm), lambda i, j: (j, 0, i)),
                   pl.BlockSpec((hps, n_keys, tm), lambda i, j: (j, 0, i)),
                   pl.BlockSpec((hps, SUBLANES, tm), lambda i, j: (j, 0, i))],
        out_shape=[jax.ShapeDtypeStruct((d, t), BF16),
                   jax.ShapeDtypeStruct((n_heads, n_keys, t), F32),
                   jax.ShapeDtypeStruct((n_heads, n_keys, t), F32),
                   jax.ShapeDtypeStruct((n_heads, SUBLANES, t), F32)],
        scratch_shapes=[pltpu.VMEM((tm, d), BF16)],
        compiler_params=_params("arbitrary", "arbitrary"),
        name="peerq",
    )(x1, mod, mod, w_q, k1, k2)


def _peer_kernel(yt_ref, dn_ref, up_ref, s1_ref, s2_ref, tc_ref, x_ref, g_ref, o_ref,
                 hd_ref, gate_ref, hst_ref, *, n_heads, n_keys, te):
    s = pl.program_id(1)
    n = pl.num_programs(1) - 1
    d, tm = yt_ref.shape

    @pl.when((pl.program_id(0) == 0) & (s == 0))
    def _():
        hst_ref[...] = jnp.zeros_like(hst_ref)

    @pl.when(s == 0)
    def _():
        o_ref[...] = jnp.zeros_like(o_ref)

    per_step = te // n_keys

    @pl.when(s < n)
    def _():
        n_kc = d // MXU_TILE
        per_a = n_kc // per_step
        tw = tm // per_a
        acc = None
        for kc in range(n_kc):
            ks = slice(MXU_TILE * kc, MXU_TILE * (kc + 1))
            part = _dot(dn_ref[:, ks], yt_ref[ks, :])
            acc = part if acc is None else acc + part
            ai, q = divmod(kc, per_a)
            a = s * per_step + ai
            cols = slice(tw * q, tw * (q + 1))
            gate = None
            for h in range(n_heads):
                cand = s1_ref[h, pl.ds(a, 1), :][:, cols] + s2_ref[h, :, cols]
                term = jnp.where(cand >= tc_ref[h, 0:1, cols], jnp.exp(cand - tc_ref[h, 1:2, cols]), 0.0)
                gate = term if gate is None else gate + term
            gate_ref[n_keys * ai:n_keys * (ai + 1), cols] = gate
        hd_ref[...] = acc

    cur = s % 2
    th = tm // 2
    pieces = [(ai, half) for ai in range(per_step) for half in range(2)]
    n_nc = d // PEER_OUT_CHUNK
    done = 0
    for j in range(n_nc):
        cols = slice(PEER_OUT_CHUNK * j, PEER_OUT_CHUNK * (j + 1))
        o_ref[:, cols] += _dot(hst_ref[1 - cur], up_ref[:, cols])
        upto = (j + 1) * len(pieces) // n_nc
        for ai, half in pieces[done:upto]:
            rows = slice(n_keys * ai, n_keys * (ai + 1))
            tok = slice(th * half, th * (half + 1))
            x = hd_ref[rows, tok]
            hs = 0.5 * x * (1.0 + lax.erf(x * (2.0 ** -0.5))) * gate_ref[rows, tok]
            hst_ref[cur, tok, rows] = hs.T.astype(BF16)
        done = upto

    @pl.when(s == n)
    def _():
        o_ref[...] = x_ref[...] + g_ref[...] * o_ref[...]
        hst_ref[1] = jnp.zeros_like(hst_ref[1])


def _peer(y_t, down, up, s1, s2, tc, x1, mod, seq, tm, te):
    d, t = y_t.shape
    n = down.shape[0] // te
    n_heads, n_keys, _ = s1.shape
    assert (d // MXU_TILE) % (te // n_keys) == 0 and d % PEER_OUT_CHUNK == 0
    once = dict(pipeline_mode=pl.Buffered(1))
    kern = functools.partial(_peer_kernel, n_heads=n_heads, n_keys=n_keys, te=te)
    return pl.pallas_call(
        kern,
        grid=(t // tm, n + 1),
        in_specs=[pl.BlockSpec((d, tm), lambda i, s: (0, i), **once),
                  pl.BlockSpec((te, d), lambda i, s: (jnp.minimum(s, n - 1), 0)),
                  pl.BlockSpec((te, d), lambda i, s: (jnp.maximum(s - 1, 0), 0)),
                  pl.BlockSpec((n_heads, n_keys, tm), lambda i, s: (0, 0, i), **once),
                  pl.BlockSpec((n_heads, n_keys, tm), lambda i, s: (0, 0, i), **once),
                  pl.BlockSpec((n_heads, SUBLANES, tm), lambda i, s: (0, 0, i), **once),
                  pl.BlockSpec((tm, d), lambda i, s: (i, 0), **once),
                  pl.BlockSpec((None, None, 1, d), lambda i, s: (5, (i * tm) // seq, 0, 0))],
        out_specs=pl.BlockSpec((tm, d), lambda i, s: (i, 0)),
        out_shape=jax.ShapeDtypeStruct((t, d), F32),
        scratch_shapes=[pltpu.VMEM((te, tm), F32), pltpu.VMEM((te, tm), F32), pltpu.VMEM((2, tm, te), BF16)],
        compiler_params=_params("arbitrary", "arbitrary"),
        name="peer",
    )(y_t, down, up, s1, s2, tc, x1, mod)


def _tile(total, want):
    return want if total % want == 0 else total


def kernel(x, c, w_ada, b_ada, w_in, w_gla_gate_up, b_gla_gate, gla_out_norm, swa_q_norm, swa_k_norm, swa_sinks, swa_out_norm, w_out, w_peer_q, peer_sub_keys_1, peer_sub_keys_2, peer_expert_down, peer_expert_up):
    batch, seq, d = x.shape
    t = batch * seq
    depth = w_ada.shape[0]
    kw = w_gla_gate_up.shape[2]
    dv = gla_out_norm.shape[1]
    dk = dv // 2
    gw = (kw // dk) * dv
    dh = swa_q_norm.shape[1]
    qw = swa_sinks.shape[1] * dh
    kvw = qw // SWA_GROUP
    in_width = w_in.shape[2]
    ga_old = 2 * kw + 2 * gw
    assert in_width == ga_old + GLA_GATE_RANK + qw + 2 * kvw
    assert gw == 2 * kw and qw == gw and d == gw + qw and 2 * dk == LANES and dv == LANES
    assert seq % SWA_BLOCK == 0 and kvw % LANES == 0
    n_keys, half = peer_sub_keys_1.shape[1:]
    n_heads = w_peer_q.shape[2] // (2 * half)
    assert n_keys == LANES and half == LANES and n_heads % 2 == 0

    q_col = ga_old
    n_tail = qw + 2 * kvw
    ga_col = ga_old + n_tail
    tn_in = 512 if ga_old % 512 == 0 and n_tail % 512 == 0 else 256
    assert ga_old % tn_in == 0 and n_tail % tn_in == 0 and qw >= tn_in

    tm_in = _tile(t, 1024)
    ts_gla = _tile(seq, 256)
    tm_out = _tile(seq, 1024)
    tm_peer = _tile(seq, 512)
    te = 512

    c_pad = jnp.pad(c, ((0, -batch % SUBLANES), (0, 0)))
    x2 = x.reshape(t, d)
    for l in range(depth):
        mod = _ada(c_pad, w_ada[l], b_ada[l][None, :])
        mod = mod[:batch].reshape(batch, N_MOD, d).transpose(1, 0, 2)[:, :, None, :]

        h = _modulate(x2, mod, seq, tm_peer)
        proj = _inproj(h, w_in[l].T, tm_in, tn_in, ga_old, n_tail)

        wup = jnp.pad(w_gla_gate_up[l], ((0, LANES - GLA_GATE_RANK), (0, 0))).astype(BF16)
        o_gla = _gla(proj, wup, b_gla_gate[l][None, :], gla_out_norm[l][None, :],
                     batch, seq, ts_gla, kw, gw, ga_col, dk, dv)
        o_swa = _swa(proj, swa_q_norm[l], swa_k_norm[l], swa_out_norm[l], swa_sinks[l],
                     batch, seq, q_col, qw, kvw, dh)
        x1 = _outproj(o_gla, o_swa, w_out[l].astype(BF16), x2, mod, seq, tm_out, 512)

        y_t, s1, s2, tc = _peerq(x1, mod, w_peer_q[l].astype(BF16), peer_sub_keys_1[l].astype(BF16),
                                 peer_sub_keys_2[l].astype(BF16), seq, tm_peer, n_heads)
        x2 = _peer(y_t, peer_expert_down[l].astype(BF16), peer_expert_up[l].astype(BF16),
                   s1, s2, tc, x1, mod, seq, tm_peer, te)
    return x2.reshape(batch, seq, d)
```

```python
import functools

import jax
import jax.numpy as jnp
from jax import lax
from jax.experimental import pallas as pl
from jax.experimental.pallas import tpu as pltpu

F32 = jnp.float32
BF16 = jnp.bfloat16

NORM_EPS = 1e-6
N_MOD = 6
GLA_GATE_RANK = 16
GLA_GATE_TEMP = 16.0
GLA_CHUNK = 64
SWA_BLOCK = 128
SWA_GROUP = 8
ROPE_THETA = 10000.0
PEER_TOPK = 16

LANES = 128
SUBLANES = 8
MXU_TILE = 256
PEER_OUT_CHUNK = 512
VMEM_LIMIT_BYTES = 60 * 1024 * 1024


def _params(*semantics):
    return pltpu.CompilerParams(dimension_semantics=semantics, vmem_limit_bytes=VMEM_LIMIT_BYTES)


def _dot(a, b):
    return jnp.dot(a, b, preferred_element_type=F32)


def _dot_nt(a, b):
    return lax.dot_general(a, b, (((1,), (1,)), ((), ())), preferred_element_type=F32)


def _dot_tn(a, b):
    return lax.dot_general(a, b, (((0,), (0,)), ((), ())), preferred_element_type=F32)


def _split(x, parts):
    out = []
    for _ in range(parts - 1):
        hi = x.astype(BF16)
        out.append(hi)
        x = x - hi.astype(F32)
    out.append(x.astype(BF16))
    return out


def _dot_split(x, m, parts):
    acc = None
    for p in _split(x, parts):
        t = _dot(p, m)
        acc = t if acc is None else acc + t
    return acc


def _modulated_rms(x, scale, shift):
    ms = jnp.mean(x * x, axis=-1, keepdims=True)
    return x * lax.rsqrt(ms + NORM_EPS) * (1.0 + scale) + shift


def _ada_kernel(c_ref, w_ref, b_ref, o_ref):
    c = c_ref[...]
    act = (c * jax.nn.sigmoid(c)).astype(BF16)
    o_ref[...] = _dot(act, w_ref[...].astype(BF16)) + b_ref[...]


def _ada(c_pad, w, b):
    rows, d = c_pad.shape
    n = w.shape[1]
    tn = 512
    return pl.pallas_call(
        _ada_kernel,
        grid=(n // tn,),
        in_specs=[pl.BlockSpec((rows, d), lambda j: (0, 0)),
                  pl.BlockSpec((d, tn), lambda j: (0, j)),
                  pl.BlockSpec((1, tn), lambda j: (0, j))],
        out_specs=pl.BlockSpec((rows, tn), lambda j: (0, j)),
        out_shape=jax.ShapeDtypeStruct((rows, n), F32),
        compiler_params=_params("arbitrary"),
        name="ada",
    )(c_pad, w, b)


def _modulate_kernel(x_ref, sh_ref, sc_ref, o_ref):
    o_ref[...] = _modulated_rms(x_ref[...], sc_ref[...], sh_ref[...]).astype(BF16)


def _modulate(x2, mod, seq, tm):
    t, d = x2.shape
    mod_spec = lambda k: pl.BlockSpec((None, None, 1, d), lambda i: (k, (i * tm) // seq, 0, 0))
    return pl.pallas_call(
        _modulate_kernel,
        grid=(t // tm,),
        in_specs=[pl.BlockSpec((tm, d), lambda i: (i, 0)), mod_spec(0), mod_spec(1)],
        out_specs=pl.BlockSpec((tm, d), lambda i: (i, 0)),
        out_shape=jax.ShapeDtypeStruct((t, d), BF16),
        compiler_params=_params("arbitrary"),
        name="modulate",
    )(x2, mod, mod)


def _inproj_kernel(h_ref, wt_ref, o_ref, w_ref):
    @pl.when(pl.program_id(1) == 0)
    def _():
        w_ref[...] = wt_ref[...].T.astype(BF16)

    o_ref[...] = _dot(h_ref[...], w_ref[...])


def _inproj(h, w_t, tm, tn, ga_old, n_tail):
    t, d = h.shape
    n_main = ga_old // tn
    n_sw = n_tail // tn

    def src_row(j):
        s = SUBLANES
        return s * jnp.where(j < n_main, j * (tn // s),
                             jnp.where(j < n_main + n_sw, (ga_old + GLA_GATE_RANK) // s + (j - n_main) * (tn // s),
                                       ga_old // s))

    n_tiles = n_main + n_sw + 1
    return pl.pallas_call(
        _inproj_kernel,
        grid=(n_tiles, t // tm),
        in_specs=[pl.BlockSpec((tm, d), lambda j, i: (i, 0)),
                  pl.BlockSpec((pl.Element(tn), pl.Element(d)), lambda j, i: (src_row(j), 0))],
        out_specs=pl.BlockSpec((tm, tn), lambda j, i: (i, j)),
        out_shape=jax.ShapeDtypeStruct((t, n_tiles * tn), F32),
        scratch_shapes=[pltpu.VMEM((d, tn), BF16)],
        compiler_params=_params("arbitrary", "arbitrary"),
        name="inproj",
    )(h, w_t)


def _gla_kernel(q_ref, k_ref, v_ref, g_ref, a_ref, wup_ref, bup_ref, gain_ref, o_ref, st_ref,
                *, n_pairs, n_chunks, dk, dv):
    @pl.when(pl.program_id(1) == 0)
    def _():
        st_ref[...] = jnp.zeros_like(st_ref)

    C = GLA_CHUNK
    row = lax.broadcasted_iota(jnp.int32, (C, C), 0)
    col = lax.broadcasted_iota(jnp.int32, (C, C), 1)
    tril = row >= col
    cum_op = tril.astype(BF16)
    causal2 = jnp.concatenate([tril, tril], axis=0)
    head0 = lax.broadcasted_iota(jnp.int32, (C, 2 * dk), 1) < dk
    head0_st = lax.broadcasted_iota(jnp.int32, (dv, 2 * dk), 1) < dk
    scale = dk ** -0.5
    gain = gain_ref[...]

    def stack_heads(x):
        return jnp.concatenate([jnp.where(head0, x, 0.0), jnp.where(head0, 0.0, x)], axis=0).astype(BF16)

    def chunk(c, carry):
        rows = pl.ds(pl.multiple_of(c * C, C), C)
        z = _dot(a_ref[rows, :].astype(BF16), wup_ref[...]) + bup_ref[...]
        log_a = (jnp.minimum(z, 0.0) - jnp.log1p(jnp.exp(-jnp.abs(z)))) * (1.0 / GLA_GATE_TEMP)
        b = _dot_split_left(cum_op, log_a)
        b_last = b[C - 1:C, :]
        mid = 0.5 * b_last
        qt = q_ref[rows, :] * scale * jnp.exp(b - mid)
        kt = k_ref[rows, :] * jnp.exp(mid - b)
        qi = qt * jnp.exp(mid)
        kl = (kt * jnp.exp(b_last - mid)).astype(BF16)
        kt = kt.astype(BF16)
        decay = jnp.exp(b_last)
        pair = lambda x, p: x[:, 2 * dk * p:2 * dk * (p + 1)]
        heads = range(2 * n_pairs)
        st = [st_ref[p] for p in range(n_pairs)]
        attn = [_dot_nt(stack_heads(pair(qt, p)), pair(kt, p)) for p in range(n_pairs)]
        inter = [_dot_nt(stack_heads(pair(qi, p)), st[p].astype(BF16)) for p in range(n_pairs)]
        attn = [jnp.where(causal2, a, 0.0).astype(BF16) for a in attn]
        v = [v_ref[rows, dv * h:dv * (h + 1)].astype(BF16) for h in heads]
        half = lambda x, h: x[h // 2][C * (h % 2):C * (h % 2 + 1)]
        out = [_dot(half(attn, h), v[h]) + half(inter, h) for h in heads]
        upd = [_dot_tn(v[h], pair(kl, h // 2)) for h in heads]
        for h in heads:
            o = out[h]
            on = o * lax.rsqrt(jnp.mean(o * o, axis=-1, keepdims=True) + NORM_EPS) * gain
            gg = g_ref[rows, dv * h:dv * (h + 1)]
            o_ref[rows, dv * h:dv * (h + 1)] = (on * (gg * jax.nn.sigmoid(gg))).astype(BF16)
        for p in range(n_pairs):
            st_ref[p] = st[p] * pair(decay, p) + jnp.where(head0_st, upd[2 * p], upd[2 * p + 1])
        return carry

    lax.fori_loop(0, n_chunks, chunk, 0)


def _dot_split_left(m, x):
    acc = None
    for p in _split(x, 3):
        t = _dot(m, p)
        acc = t if acc is None else acc + t
    return acc


def _gla(proj, wup, bup, gain, batch, seq, ts, kw, gw, ga_col, dk, dv):
    t = proj.shape[0]
    nb = seq // ts
    rb = lambda b, s: b * nb + s
    kern = functools.partial(_gla_kernel, n_pairs=kw // (2 * dk), n_chunks=ts // GLA_CHUNK, dk=dk, dv=dv)
    return pl.pallas_call(
        kern,
        grid=(batch, nb),
        in_specs=[pl.BlockSpec((ts, kw), lambda b, s: (rb(b, s), 0)),
                  pl.BlockSpec((ts, kw), lambda b, s: (rb(b, s), 1)),
                  pl.BlockSpec((ts, gw), lambda b, s: (rb(b, s), (2 * kw) // gw)),
                  pl.BlockSpec((ts, gw), lambda b, s: (rb(b, s), (2 * kw) // gw + 1)),
                  pl.BlockSpec((ts, LANES), lambda b, s: (rb(b, s), ga_col // LANES)),
                  pl.BlockSpec(wup.shape, lambda b, s: (0, 0)),
                  pl.BlockSpec(bup.shape, lambda b, s: (0, 0)),
                  pl.BlockSpec(gain.shape, lambda b, s: (0, 0))],
        out_specs=pl.BlockSpec((ts, gw), lambda b, s: (rb(b, s), 0)),
        out_shape=jax.ShapeDtypeStruct((t, gw), BF16),
        scratch_shapes=[pltpu.VMEM((kw // (2 * dk), dv, 2 * dk), F32)],
        compiler_params=_params("arbitrary", "arbitrary"),
        name="gla",
    )(proj, proj, proj, proj, proj, wup, bup, gain)


def _group_rms(x, ones_bd, dh):
    ss = _dot_split(x * x, ones_bd, 2)
    return x * lax.rsqrt(ss * (1.0 / dh) + NORM_EPS)


def _rope(x, cos, sin_signed, dh):
    half = dh // 2
    width = x.shape[1]
    first_half = (lax.broadcasted_iota(jnp.int32, x.shape, 1) & (dh - 1)) < half
    partner = jnp.where(first_half, pltpu.roll(x, width - half, 1), pltpu.roll(x, half, 1))
    return x * cos + partner * sin_signed


def _swa_kernel(q_ref, kc_ref, kp_ref, vc_ref, vp_ref, cc_ref, cp_ref, sc_ref, sp_ref,
                gq_ref, gk_ref, go_ref, sink_ref, ones_ref, sel_ref, tab_ref, o_ref, tab_out_ref,
                *, n_groups, dh):
    W, G = SWA_BLOCK, SWA_GROUP
    n = pl.program_id(1)
    gw = G * dh
    kvw = n_groups * dh
    pairs = G // 2
    tab_out_ref[...] = tab_ref[...].astype(BF16)

    k_all = jnp.concatenate([kp_ref[...], kc_ref[...]], axis=0)
    cos_k = jnp.concatenate([cp_ref[:, :kvw], cc_ref[:, :kvw]], axis=0)
    sin_k = jnp.concatenate([sp_ref[:, :kvw], sc_ref[:, :kvw]], axis=0)
    k_r = _rope(_group_rms(k_all, ones_ref[:kvw, :kvw], dh) * gk_ref[...], cos_k, sin_k, dh)
    k_t = k_r.T.astype(BF16)
    v_all = jnp.concatenate([vp_ref[...], vc_ref[...]], axis=0).astype(BF16)

    qi = lax.broadcasted_iota(jnp.int32, (2 * W, 2 * W), 0) & (W - 1)
    kj = lax.broadcasted_iota(jnp.int32, (2 * W, 2 * W), 1)
    rel = kj - qi - 1
    first_key = jnp.where(n > 0, 0, W)
    valid = (rel >= 0) & (rel < W) & (kj >= first_key)
    head0 = lax.broadcasted_iota(jnp.int32, (W, 2 * dh), 1) < dh
    ones_cols = jnp.ones((2 * W, 2 * dh), BF16)

    for g in range(n_groups):
        q_n = _group_rms(q_ref[:, gw * g:gw * (g + 1)], ones_ref[...], dh) * gq_ref[...]
        q_r = _rope(q_n, cc_ref[...], sc_ref[...], dh)
        k_tg = k_t[dh * g:dh * (g + 1), :]
        k_t2 = jnp.concatenate([k_tg, k_tg], axis=0)
        v3 = jnp.concatenate([_dot(v_all, sel_ref[g]).astype(BF16), ones_cols], axis=1)
        outs = []
        for pp in range(pairs):
            qp = q_r[:, 2 * dh * pp:2 * dh * (pp + 1)]
            q2 = jnp.concatenate([jnp.where(head0, qp, 0.0), jnp.where(head0, 0.0, qp)], axis=0).astype(BF16)
            s = jnp.where(valid, _dot(q2, k_t2) * (dh ** -0.5), -jnp.inf)
            sink = sink_ref[pairs * g + pp]
            m = jnp.maximum(jnp.max(s, axis=-1, keepdims=True), sink)
            r = _dot(jnp.exp(s - m).astype(BF16), v3)
            o2 = r[:, :2 * dh] / (r[:, 2 * dh:] + jnp.exp(sink - m))
            outs.append(jnp.where(head0, o2[:W], o2[W:]))
        o = jnp.concatenate(outs, axis=1)
        o_ref[:, gw * g:gw * (g + 1)] = (_group_rms(o, ones_ref[...], dh) * go_ref[...]).astype(BF16)


def _swa_constants(dh, n_groups, seq):
    G = SWA_GROUP
    gw = G * dh
    half = dh // 2
    i = jnp.arange(gw)
    ones_bd = (i[:, None] // dh == i[None, :] // dh).astype(BF16)
    kvw = n_groups * dh
    j = jnp.arange(2 * dh)
    sel = (jnp.arange(kvw)[None, :, None] == (dh * jnp.arange(n_groups)[:, None, None] + (j % dh)[None, None, :])).astype(BF16)
    inv = ROPE_THETA ** (-jnp.arange(half, dtype=F32) / half)
    ang = jnp.arange(seq, dtype=F32)[:, None] * inv[None, :]
    cos = jnp.tile(jnp.cos(ang), (1, 2 * G))
    sin = jnp.tile(jnp.concatenate([-jnp.sin(ang), jnp.sin(ang)], axis=1), (1, G))
    return ones_bd, sel, cos, sin


def _swa(proj, q_gain, k_gain, o_gain, sinks, table, batch, seq, q_col, qw, kvw, dh):
    t = proj.shape[0]
    tab_rows = table.shape[0] // (batch * (seq // SWA_BLOCK))
    assert tab_rows * batch * (seq // SWA_BLOCK) == table.shape[0] and tab_rows % (2 * SUBLANES) == 0
    tab_spec = pl.BlockSpec((tab_rows, table.shape[1]), lambda b, s: (b * (seq // SWA_BLOCK) + s, 0))
    W, G = SWA_BLOCK, SWA_GROUP
    gw = G * dh
    n_groups = kvw // dh
    nb = seq // W
    assert dh & (dh - 1) == 0 and 2 * dh == LANES
    ones_bd, sel, cos, sin = _swa_constants(dh, n_groups, seq)
    gq = jnp.tile(q_gain, G)[None, :]
    gk = jnp.tile(k_gain, n_groups)[None, :]
    go = jnp.tile(o_gain, G)[None, :]
    sink_cols = jnp.repeat(sinks.reshape(-1, 2), W, axis=1)[:, :, None]
    cur = lambda b, s: b * nb + s
    prev = lambda b, s: b * nb + jnp.maximum(s - 1, 0)
    k_blk = (q_col + qw) // kvw
    const = lambda a: pl.BlockSpec(a.shape, lambda b, s: (0,) * a.ndim)
    kern = functools.partial(_swa_kernel, n_groups=n_groups, dh=dh)
    return pl.pallas_call(
        kern,
        grid=(batch, nb),
        in_specs=[pl.BlockSpec((W, qw), lambda b, s: (cur(b, s), q_col // qw)),
                  pl.BlockSpec((W, kvw), lambda b, s: (cur(b, s), k_blk)),
                  pl.BlockSpec((W, kvw), lambda b, s: (prev(b, s), k_blk)),
                  pl.BlockSpec((W, kvw), lambda b, s: (cur(b, s), k_blk + 1)),
                  pl.BlockSpec((W, kvw), lambda b, s: (prev(b, s), k_blk + 1)),
                  pl.BlockSpec((W, gw), lambda b, s: (s, 0)),
                  pl.BlockSpec((W, gw), lambda b, s: (jnp.maximum(s - 1, 0), 0)),
                  pl.BlockSpec((W, gw), lambda b, s: (s, 0)),
                  pl.BlockSpec((W, gw), lambda b, s: (jnp.maximum(s - 1, 0), 0)),
                  const(gq), const(gk), const(go), const(sink_cols), const(ones_bd), const(sel), tab_spec],
        out_specs=[pl.BlockSpec((W, qw), lambda b, s: (cur(b, s), 0)), tab_spec],
        out_shape=[jax.ShapeDtypeStruct((t, qw), BF16), jax.ShapeDtypeStruct(table.shape, BF16)],
        compiler_params=_params("arbitrary", "arbitrary"),
        name="swa",
    )(proj, proj, proj, proj, proj, cos, cos, sin, sin, gq, gk, go, sink_cols, ones_bd, sel, table)


def _outproj_kernel(og_ref, os_ref, wg_ref, ws_ref, x_ref, g_ref, o_ref, wgb_ref, wsb_ref):
    @pl.when(pl.program_id(1) == 0)
    def _():
        wgb_ref[...] = wg_ref[...].astype(BF16)
        wsb_ref[...] = ws_ref[...].astype(BF16)

    mix = _dot(og_ref[...], wgb_ref[...]) + _dot(os_ref[...], wsb_ref[...])
    o_ref[...] = x_ref[...] + g_ref[...] * mix


def _outproj(o_gla, o_swa, w_out, x2, mod, seq, tm, tn):
    t, d = x2.shape
    half = o_gla.shape[1]
    return pl.pallas_call(
        _outproj_kernel,
        grid=(d // tn, t // tm),
        in_specs=[pl.BlockSpec((tm, half), lambda j, i: (i, 0)),
                  pl.BlockSpec((tm, half), lambda j, i: (i, 0)),
                  pl.BlockSpec((half, tn), lambda j, i: (0, j)),
                  pl.BlockSpec((half, tn), lambda j, i: (1, j)),
                  pl.BlockSpec((tm, tn), lambda j, i: (i, j)),
                  pl.BlockSpec((None, None, 1, tn), lambda j, i: (2, (i * tm) // seq, 0, j))],
        out_specs=pl.BlockSpec((tm, tn), lambda j, i: (i, j)),
        out_shape=jax.ShapeDtypeStruct((t, d), F32),
        scratch_shapes=[pltpu.VMEM((half, tn), BF16), pltpu.VMEM((half, tn), BF16)],
        compiler_params=_params("arbitrary", "arbitrary"),
        name="outproj",
    )(o_gla, o_swa, w_out, w_out, x2, mod)


def _top_values(s, k):
    vals = []
    for _ in range(k):
        m = jnp.max(s, axis=0, keepdims=True)
        vals.append(m)
        s = jnp.where(s == m, -jnp.inf, s)
    return vals


def _peerq_kernel(x_ref, sh_ref, sc_ref, w_ref, k1_ref, k2_ref, tab_ref, yt_ref, s1_ref, s2_ref, tc_ref,
                  tab_out_ref, y_ref, *, heads_per_step, half):
    tab_out_ref[...] = tab_ref[...].astype(BF16)

    @pl.when(pl.program_id(1) == 0)
    def _():
        y = _modulated_rms(x_ref[...], sc_ref[...], sh_ref[...])
        y_ref[...] = y.astype(BF16)
        yt_ref[...] = y.T.astype(BF16)

    K = PEER_TOPK
    q = _dot(y_ref[...], w_ref[...])
    tm = q.shape[0]
    for hh in range(heads_per_step):
        q1 = q[:, 2 * half * hh:2 * half * hh + half].astype(BF16)
        q2 = q[:, 2 * half * hh + half:2 * half * (hh + 1)].astype(BF16)
        s1 = _dot_nt(k1_ref[...], q1)
        s2 = _dot_nt(k2_ref[...], q2)
        s1_ref[hh] = s1
        s2_ref[hh] = s2
        v1 = _top_values(s1, K)
        v2 = jnp.concatenate(_top_values(s2, K), axis=0)
        cand = [v1[0] + v2]
        for i in range(1, K // 2):
            cand.append(v1[i] + v2[:K // 2])
        cand.append(jnp.concatenate(v1[K // 2:], axis=0) + v2[0:1])
        top = _top_values(jnp.concatenate(cand, axis=0), K)
        m = top[0]
        z = jnp.exp(top[0] - m)
        for r in range(1, K):
            z = z + jnp.exp(top[r] - m)
        tc_ref[hh] = jnp.concatenate([top[K - 1], m + jnp.log(z), jnp.zeros((SUBLANES - 2, tm), F32)], axis=0)


def _peerq(x1, mod, w_q, k1, k2, table, seq, tm, n_heads):
    t, d = x1.shape
    nq = w_q.shape[1]
    n_keys, half = k1.shape
    hps = 2
    tn = hps * 2 * half
    n_j = nq // tn
    tab_rows = table.shape[0] // ((t // tm) * n_j)
    assert tab_rows * (t // tm) * n_j == table.shape[0] and tab_rows % (2 * SUBLANES) == 0
    tab_spec = pl.BlockSpec((tab_rows, table.shape[1]), lambda i, j: (i * n_j + j, 0))
    mod_spec = lambda k: pl.BlockSpec((None, None, 1, d), lambda i, j: (k, (i * tm) // seq, 0, 0))
    kern = functools.partial(_peerq_kernel, heads_per_step=hps, half=half)
    return pl.pallas_call(
        kern,
        grid=(t // tm, n_j),
        in_specs=[pl.BlockSpec((tm, d), lambda i, j: (i, 0)),
                  mod_spec(3), mod_spec(4),
                  pl.BlockSpec((d, tn), lambda i, j: (0, j)),
                  pl.BlockSpec(k1.shape, lambda i, j: (0, 0)),
                  pl.BlockSpec(k2.shape, lambda i, j: (0, 0)),
                  tab_spec],
        out_specs=[pl.BlockSpec((d, tm), lambda i, j: (0, i)),
                   pl.BlockSpec((hps, n_keys, tm), lambda i, j: (j, 0, i)),
                   pl.BlockSpec((hps, n_keys, tm), lambda i, j: (j, 0, i)),
                   pl.BlockSpec((hps, SUBLANES, tm), lambda i, j: (j, 0, i)),
                   tab_spec],
        out_shape=[jax.ShapeDtypeStruct((d, t), BF16),
                   jax.ShapeDtypeStruct((n_heads, n_keys, t), F32),
                   jax.ShapeDtypeStruct((n_heads, n_keys, t), F32),
                   jax.ShapeDtypeStruct((n_heads, SUBLANES, t), F32),
                   jax.ShapeDtypeStruct(table.shape, BF16)],
        scratch_shapes=[pltpu.VMEM((tm, d), BF16)],
        compiler_params=_params("arbitrary", "arbitrary"),
        name="peerq",
    )(x1, mod, mod, w_q, k1, k2, table)


def _peer_kernel(yt_ref, dn_ref, up_ref, s1_ref, s2_ref, tc_ref, x_ref, g_ref, o_ref,
                 hd_ref, gate_ref, hst_ref, *, n_heads, n_keys, te):
    s = pl.program_id(1)
    n = pl.num_programs(1) - 1
    d, tm = yt_ref.shape

    @pl.when((pl.program_id(0) == 0) & (s == 0))
    def _():
        hst_ref[...] = jnp.zeros_like(hst_ref)

    @pl.when(s == 0)
    def _():
        o_ref[...] = jnp.zeros_like(o_ref)

    per_step = te // n_keys

    @pl.when(s < n)
    def _():
        n_kc = d // MXU_TILE
        per_a = n_kc // per_step
        tw = tm // per_a
        acc = None
        for kc in range(n_kc):
            ks = slice(MXU_TILE * kc, MXU_TILE * (kc + 1))
            part = _dot(dn_ref[:, ks], yt_ref[ks, :])
            acc = part if acc is None else acc + part
            ai, q = divmod(kc, per_a)
            a = s * per_step + ai
            cols = slice(tw * q, tw * (q + 1))
            gate = None
            for h in range(n_heads):
                cand = s1_ref[h, pl.ds(a, 1), :][:, cols] + s2_ref[h, :, cols]
                term = jnp.where(cand >= tc_ref[h, 0:1, cols], jnp.exp(cand - tc_ref[h, 1:2, cols]), 0.0)
                gate = term if gate is None else gate + term
            gate_ref[n_keys * ai:n_keys * (ai + 1), cols] = gate
        hd_ref[...] = acc

    cur = s % 2
    th = tm // 2
    pieces = [(ai, half) for ai in range(per_step) for half in range(2)]
    n_nc = d // PEER_OUT_CHUNK
    done = 0
    for j in range(n_nc):
        cols = slice(PEER_OUT_CHUNK * j, PEER_OUT_CHUNK * (j + 1))
        o_ref[:, cols] += _dot(hst_ref[1 - cur], up_ref[:, cols])
        upto = (j + 1) * len(pieces) // n_nc
        for ai, half in pieces[done:upto]:
            rows = slice(n_keys * ai, n_keys * (ai + 1))
            tok = slice(th * half, th * (half + 1))
            x = hd_ref[rows, tok]
            hs = 0.5 * x * (1.0 + lax.erf(x * (2.0 ** -0.5))) * gate_ref[rows, tok]
            hst_ref[cur, tok, rows] = hs.T.astype(BF16)
        done = upto

    @pl.when(s == n)
    def _():
        o_ref[...] = x_ref[...] + g_ref[...] * o_ref[...]
        hst_ref[1] = jnp.zeros_like(hst_ref[1])


def _peer(y_t, down, up, s1, s2, tc, x1, mod, seq, tm, te):
    d, t = y_t.shape
    n = down.shape[0] // te
    n_heads, n_keys, _ = s1.shape
    assert (d // MXU_TILE) % (te // n_keys) == 0 and d % PEER_OUT_CHUNK == 0
    once = dict(pipeline_mode=pl.Buffered(1))
    kern = functools.partial(_peer_kernel, n_heads=n_heads, n_keys=n_keys, te=te)
    return pl.pallas_call(
        kern,
        grid=(t // tm, n + 1),
        in_specs=[pl.BlockSpec((d, tm), lambda i, s: (0, i), **once),
                  pl.BlockSpec((te, d), lambda i, s: (jnp.minimum(s, n - 1), 0)),
                  pl.BlockSpec((te, d), lambda i, s: (jnp.maximum(s - 1, 0), 0)),
                  pl.BlockSpec((n_heads, n_keys, tm), lambda i, s: (0, 0, i), **once),
                  pl.BlockSpec((n_heads, n_keys, tm), lambda i, s: (0, 0, i), **once),
                  pl.BlockSpec((n_heads, SUBLANES, tm), lambda i, s: (0, 0, i), **once),
                  pl.BlockSpec((tm, d), lambda i, s: (i, 0), **once),
                  pl.BlockSpec((None, None, 1, d), lambda i, s: (5, (i * tm) // seq, 0, 0))],
        out_specs=pl.BlockSpec((tm, d), lambda i, s: (i, 0)),
        out_shape=jax.ShapeDtypeStruct((t, d), F32),
        scratch_shapes=[pltpu.VMEM((te, tm), F32), pltpu.VMEM((te, tm), F32), pltpu.VMEM((2, tm, te), BF16)],
        compiler_params=_params("arbitrary", "arbitrary"),
        name="peer",
    )(y_t, down, up, s1, s2, tc, x1, mod)


def _tile(total, want):
    return want if total % want == 0 else total


def kernel(x, c, w_ada, b_ada, w_in, w_gla_gate_up, b_gla_gate, gla_out_norm, swa_q_norm, swa_k_norm, swa_sinks, swa_out_norm, w_out, w_peer_q, peer_sub_keys_1, peer_sub_keys_2, peer_expert_down, peer_expert_up):
    batch, seq, d = x.shape
    t = batch * seq
    depth = w_ada.shape[0]
    kw = w_gla_gate_up.shape[2]
    dv = gla_out_norm.shape[1]
    dk = dv // 2
    gw = (kw // dk) * dv
    dh = swa_q_norm.shape[1]
    qw = swa_sinks.shape[1] * dh
    kvw = qw // SWA_GROUP
    in_width = w_in.shape[2]
    ga_old = 2 * kw + 2 * gw
    assert in_width == ga_old + GLA_GATE_RANK + qw + 2 * kvw
    assert gw == 2 * kw and qw == gw and d == gw + qw and 2 * dk == LANES and dv == LANES
    assert seq % SWA_BLOCK == 0 and kvw % LANES == 0
    n_keys, half = peer_sub_keys_1.shape[1:]
    n_heads = w_peer_q.shape[2] // (2 * half)
    assert n_keys == LANES and half == LANES and n_heads % 2 == 0

    q_col = ga_old
    n_tail = qw + 2 * kvw
    ga_col = ga_old + n_tail
    tn_in = 512 if ga_old % 512 == 0 and n_tail % 512 == 0 else 256
    assert ga_old % tn_in == 0 and n_tail % tn_in == 0 and qw >= tn_in

    tm_in = _tile(t, 1024)
    ts_gla = _tile(seq, 256)
    tm_out = _tile(seq, 1024)
    tm_peer = _tile(seq, 512)
    te = 512

    c_pad = jnp.pad(c, ((0, -batch % SUBLANES), (0, 0)))
    x2 = x.reshape(t, d)
    for l in range(depth):
        mod = _ada(c_pad, w_ada[l], b_ada[l][None, :])
        mod = mod[:batch].reshape(batch, N_MOD, d).transpose(1, 0, 2)[:, :, None, :]

        h = _modulate(x2, mod, seq, tm_peer)
        proj = _inproj(h, w_in[l].T, tm_in, tn_in, ga_old, n_tail)

        wup = jnp.pad(w_gla_gate_up[l], ((0, LANES - GLA_GATE_RANK), (0, 0))).astype(BF16)
        o_gla = _gla(proj, wup, b_gla_gate[l][None, :], gla_out_norm[l][None, :],
                     batch, seq, ts_gla, kw, gw, ga_col, dk, dv)
        o_swa, down = _swa(proj, swa_q_norm[l], swa_k_norm[l], swa_out_norm[l], swa_sinks[l],
                           peer_expert_down[l], batch, seq, q_col, qw, kvw, dh)
        x1 = _outproj(o_gla, o_swa, w_out[l], x2, mod, seq, tm_out, 512)

        y_t, s1, s2, tc, up = _peerq(x1, mod, w_peer_q[l].astype(BF16), peer_sub_keys_1[l].astype(BF16),
                                     peer_sub_keys_2[l].astype(BF16), peer_expert_up[l], seq, tm_peer, n_heads)
        x2 = _peer(y_t, down, up, s1, s2, tc, x1, mod, seq, tm_peer, te)
    return x2.reshape(batch, seq, d)
```

```python
import functools

import jax
import jax.numpy as jnp
from jax import lax
from jax.experimental import pallas as pl
from jax.experimental.pallas import tpu as pltpu

F32 = jnp.float32
BF16 = jnp.bfloat16
F8 = jnp.float8_e4m3fn
F8_MAX = 448.0

NORM_EPS = 1e-6
N_MOD = 6
GLA_GATE_RANK = 16
GLA_GATE_TEMP = 16.0
GLA_CHUNK = 64
SWA_BLOCK = 128
SWA_GROUP = 8
ROPE_THETA = 10000.0
PEER_TOPK = 16

LANES = 128
SUBLANES = 8
MXU_TILE = 256
PEER_OUT_CHUNK = 512
VMEM_LIMIT_BYTES = 60 * 1024 * 1024


def _params(*semantics):
    return pltpu.CompilerParams(dimension_semantics=semantics, vmem_limit_bytes=VMEM_LIMIT_BYTES)


def _dot(a, b):
    return jnp.dot(a, b, preferred_element_type=F32)


def _dot_nt(a, b):
    return lax.dot_general(a, b, (((1,), (1,)), ((), ())), preferred_element_type=F32)


def _dot_tn(a, b):
    return lax.dot_general(a, b, (((0,), (0,)), ((), ())), preferred_element_type=F32)


def _split(x, parts):
    out = []
    for _ in range(parts - 1):
        hi = x.astype(BF16)
        out.append(hi)
        x = x - hi.astype(F32)
    out.append(x.astype(BF16))
    return out


def _dot_split(x, m, parts):
    acc = None
    for p in _split(x, parts):
        t = _dot(p, m)
        acc = t if acc is None else acc + t
    return acc


def _quantize_rows(x, axis):
    amax = jnp.max(jnp.abs(x), axis=axis, keepdims=True)
    scale = jnp.maximum(amax, jnp.finfo(F32).tiny) * (1.0 / F8_MAX)
    return (x * (1.0 / scale)).astype(F8), scale


def _modulated_rms(x, scale, shift):
    ms = jnp.mean(x * x, axis=-1, keepdims=True)
    return x * lax.rsqrt(ms + NORM_EPS) * (1.0 + scale) + shift


def _ada_kernel(c_ref, w_ref, b_ref, o_ref):
    c = c_ref[...]
    act = (c * jax.nn.sigmoid(c)).astype(BF16)
    o_ref[...] = _dot(act, w_ref[...].astype(BF16)) + b_ref[...]


def _ada(c_pad, w, b):
    rows, d = c_pad.shape
    n = w.shape[1]
    tn = 512
    return pl.pallas_call(
        _ada_kernel,
        grid=(n // tn,),
        in_specs=[pl.BlockSpec((rows, d), lambda j: (0, 0)),
                  pl.BlockSpec((d, tn), lambda j: (0, j)),
                  pl.BlockSpec((1, tn), lambda j: (0, j))],
        out_specs=pl.BlockSpec((rows, tn), lambda j: (0, j)),
        out_shape=jax.ShapeDtypeStruct((rows, n), F32),
        compiler_params=_params("arbitrary"),
        name="ada",
    )(c_pad, w, b)


def _modulate_kernel(x_ref, sh_ref, sc_ref, o_ref):
    o_ref[...] = _modulated_rms(x_ref[...], sc_ref[...], sh_ref[...]).astype(BF16)


def _modulate(x2, mod, seq, tm):
    t, d = x2.shape
    mod_spec = lambda k: pl.BlockSpec((None, None, 1, d), lambda i: (k, (i * tm) // seq, 0, 0))
    return pl.pallas_call(
        _modulate_kernel,
        grid=(t // tm,),
        in_specs=[pl.BlockSpec((tm, d), lambda i: (i, 0)), mod_spec(0), mod_spec(1)],
        out_specs=pl.BlockSpec((tm, d), lambda i: (i, 0)),
        out_shape=jax.ShapeDtypeStruct((t, d), BF16),
        compiler_params=_params("arbitrary"),
        name="modulate",
    )(x2, mod, mod)


def _inproj_kernel(h_ref, wt_ref, o_ref, w_ref):
    @pl.when(pl.program_id(1) == 0)
    def _():
        w_ref[...] = wt_ref[...].T.astype(BF16)

    o_ref[...] = _dot(h_ref[...], w_ref[...])


def _inproj(h, w_t, tm, tn, ga_old, n_tail):
    t, d = h.shape
    n_main = ga_old // tn
    n_sw = n_tail // tn

    def src_row(j):
        s = SUBLANES
        return s * jnp.where(j < n_main, j * (tn // s),
                             jnp.where(j < n_main + n_sw, (ga_old + GLA_GATE_RANK) // s + (j - n_main) * (tn // s),
                                       ga_old // s))

    n_tiles = n_main + n_sw + 1
    return pl.pallas_call(
        _inproj_kernel,
        grid=(n_tiles, t // tm),
        in_specs=[pl.BlockSpec((tm, d), lambda j, i: (i, 0)),
                  pl.BlockSpec((pl.Element(tn), pl.Element(d)), lambda j, i: (src_row(j), 0))],
        out_specs=pl.BlockSpec((tm, tn), lambda j, i: (i, j)),
        out_shape=jax.ShapeDtypeStruct((t, n_tiles * tn), F32),
        scratch_shapes=[pltpu.VMEM((d, tn), BF16)],
        compiler_params=_params("arbitrary", "arbitrary"),
        name="inproj",
    )(h, w_t)


def _gla_kernel(q_ref, k_ref, v_ref, g_ref, a_ref, wup_ref, bup_ref, gain_ref, o_ref, st_ref,
                *, n_pairs, n_chunks, dk, dv):
    @pl.when(pl.program_id(1) == 0)
    def _():
        st_ref[...] = jnp.zeros_like(st_ref)

    C = GLA_CHUNK
    row = lax.broadcasted_iota(jnp.int32, (C, C), 0)
    col = lax.broadcasted_iota(jnp.int32, (C, C), 1)
    tril = row >= col
    cum_op = tril.astype(BF16)
    causal2 = jnp.concatenate([tril, tril], axis=0)
    head0 = lax.broadcasted_iota(jnp.int32, (C, 2 * dk), 1) < dk
    head0_st = lax.broadcasted_iota(jnp.int32, (dv, 2 * dk), 1) < dk
    scale = dk ** -0.5
    gain = gain_ref[...]

    def stack_heads(x):
        return jnp.concatenate([jnp.where(head0, x, 0.0), jnp.where(head0, 0.0, x)], axis=0).astype(BF16)

    def chunk(c, carry):
        rows = pl.ds(pl.multiple_of(c * C, C), C)
        z = _dot(a_ref[rows, :].astype(BF16), wup_ref[...]) + bup_ref[...]
        log_a = (jnp.minimum(z, 0.0) - jnp.log1p(jnp.exp(-jnp.abs(z)))) * (1.0 / GLA_GATE_TEMP)
        b = _dot_split_left(cum_op, log_a)
        b_last = b[C - 1:C, :]
        mid = 0.5 * b_last
        qt = q_ref[rows, :] * scale * jnp.exp(b - mid)
        kt = k_ref[rows, :] * jnp.exp(mid - b)
        qi = qt * jnp.exp(mid)
        kl = (kt * jnp.exp(b_last - mid)).astype(BF16)
        kt = kt.astype(BF16)
        decay = jnp.exp(b_last)
        pair = lambda x, p: x[:, 2 * dk * p:2 * dk * (p + 1)]
        heads = range(2 * n_pairs)
        st = [st_ref[p] for p in range(n_pairs)]
        attn = [_dot_nt(stack_heads(pair(qt, p)), pair(kt, p)) for p in range(n_pairs)]
        inter = [_dot_nt(stack_heads(pair(qi, p)), st[p].astype(BF16)) for p in range(n_pairs)]
        attn = [jnp.where(causal2, a, 0.0).astype(BF16) for a in attn]
        v = [v_ref[rows, dv * h:dv * (h + 1)].astype(BF16) for h in heads]
        half = lambda x, h: x[h // 2][C * (h % 2):C * (h % 2 + 1)]
        out = [_dot(half(attn, h), v[h]) + half(inter, h) for h in heads]
        upd = [_dot_tn(v[h], pair(kl, h // 2)) for h in heads]
        for h in heads:
            o = out[h]
            on = o * lax.rsqrt(jnp.mean(o * o, axis=-1, keepdims=True) + NORM_EPS) * gain
            gg = g_ref[rows, dv * h:dv * (h + 1)]
            o_ref[rows, dv * h:dv * (h + 1)] = (on * (gg * jax.nn.sigmoid(gg))).astype(BF16)
        for p in range(n_pairs):
            st_ref[p] = st[p] * pair(decay, p) + jnp.where(head0_st, upd[2 * p], upd[2 * p + 1])
        return carry

    lax.fori_loop(0, n_chunks, chunk, 0)


def _dot_split_left(m, x):
    acc = None
    for p in _split(x, 3):
        t = _dot(m, p)
        acc = t if acc is None else acc + t
    return acc


def _gla(proj, wup, bup, gain, batch, seq, ts, kw, gw, ga_col, dk, dv):
    t = proj.shape[0]
    nb = seq // ts
    rb = lambda b, s: b * nb + s
    kern = functools.partial(_gla_kernel, n_pairs=kw // (2 * dk), n_chunks=ts // GLA_CHUNK, dk=dk, dv=dv)
    return pl.pallas_call(
        kern,
        grid=(batch, nb),
        in_specs=[pl.BlockSpec((ts, kw), lambda b, s: (rb(b, s), 0)),
                  pl.BlockSpec((ts, kw), lambda b, s: (rb(b, s), 1)),
                  pl.BlockSpec((ts, gw), lambda b, s: (rb(b, s), (2 * kw) // gw)),
                  pl.BlockSpec((ts, gw), lambda b, s: (rb(b, s), (2 * kw) // gw + 1)),
                  pl.BlockSpec((ts, LANES), lambda b, s: (rb(b, s), ga_col // LANES)),
                  pl.BlockSpec(wup.shape, lambda b, s: (0, 0)),
                  pl.BlockSpec(bup.shape, lambda b, s: (0, 0)),
                  pl.BlockSpec(gain.shape, lambda b, s: (0, 0))],
        out_specs=pl.BlockSpec((ts, gw), lambda b, s: (rb(b, s), 0)),
        out_shape=jax.ShapeDtypeStruct((t, gw), BF16),
        scratch_shapes=[pltpu.VMEM((kw // (2 * dk), dv, 2 * dk), F32)],
        compiler_params=_params("arbitrary", "arbitrary"),
        name="gla",
    )(proj, proj, proj, proj, proj, wup, bup, gain)


def _group_rms(x, ones_bd, dh):
    ss = _dot_split(x * x, ones_bd, 2)
    return x * lax.rsqrt(ss * (1.0 / dh) + NORM_EPS)


def _rope(x, cos, sin_signed, dh):
    half = dh // 2
    width = x.shape[1]
    first_half = (lax.broadcasted_iota(jnp.int32, x.shape, 1) & (dh - 1)) < half
    partner = jnp.where(first_half, pltpu.roll(x, width - half, 1), pltpu.roll(x, half, 1))
    return x * cos + partner * sin_signed


def _swa_kernel(q_ref, kc_ref, kp_ref, vc_ref, vp_ref, cc_ref, cp_ref, sc_ref, sp_ref,
                gq_ref, gk_ref, go_ref, sink_ref, ones_ref, sel_ref, tab_ref, o_ref, tab_out_ref, tab_scale_ref,
                *, n_groups, dh):
    W, G = SWA_BLOCK, SWA_GROUP
    n = pl.program_id(1)
    gw = G * dh
    kvw = n_groups * dh
    pairs = G // 2
    tab_out_ref[...], scale = _quantize_rows(tab_ref[...], axis=1)
    tab_scale_ref[...] = jnp.broadcast_to(scale, tab_scale_ref.shape)

    k_all = jnp.concatenate([kp_ref[...], kc_ref[...]], axis=0)
    cos_k = jnp.concatenate([cp_ref[:, :kvw], cc_ref[:, :kvw]], axis=0)
    sin_k = jnp.concatenate([sp_ref[:, :kvw], sc_ref[:, :kvw]], axis=0)
    k_r = _rope(_group_rms(k_all, ones_ref[:kvw, :kvw], dh) * gk_ref[...], cos_k, sin_k, dh)
    k_t = k_r.T.astype(BF16)
    v_all = jnp.concatenate([vp_ref[...], vc_ref[...]], axis=0).astype(BF16)

    qi = lax.broadcasted_iota(jnp.int32, (2 * W, 2 * W), 0) & (W - 1)
    kj = lax.broadcasted_iota(jnp.int32, (2 * W, 2 * W), 1)
    rel = kj - qi - 1
    first_key = jnp.where(n > 0, 0, W)
    valid = (rel >= 0) & (rel < W) & (kj >= first_key)
    head0 = lax.broadcasted_iota(jnp.int32, (W, 2 * dh), 1) < dh
    ones_cols = jnp.ones((2 * W, 2 * dh), BF16)

    for g in range(n_groups):
        q_n = _group_rms(q_ref[:, gw * g:gw * (g + 1)], ones_ref[...], dh) * gq_ref[...]
        q_r = _rope(q_n, cc_ref[...], sc_ref[...], dh)
        k_tg = k_t[dh * g:dh * (g + 1), :]
        k_t2 = jnp.concatenate([k_tg, k_tg], axis=0)
        v3 = jnp.concatenate([_dot(v_all, sel_ref[g]).astype(BF16), ones_cols], axis=1)
        outs = []
        for pp in range(pairs):
            qp = q_r[:, 2 * dh * pp:2 * dh * (pp + 1)]
            q2 = jnp.concatenate([jnp.where(head0, qp, 0.0), jnp.where(head0, 0.0, qp)], axis=0).astype(BF16)
            s = jnp.where(valid, _dot(q2, k_t2) * (dh ** -0.5), -jnp.inf)
            sink = sink_ref[pairs * g + pp]
            m = jnp.maximum(jnp.max(s, axis=-1, keepdims=True), sink)
            r = _dot(jnp.exp(s - m).astype(BF16), v3)
            o2 = r[:, :2 * dh] / (r[:, 2 * dh:] + jnp.exp(sink - m))
            outs.append(jnp.where(head0, o2[:W], o2[W:]))
        o = jnp.concatenate(outs, axis=1)
        o_ref[:, gw * g:gw * (g + 1)] = (_group_rms(o, ones_ref[...], dh) * go_ref[...]).astype(BF16)


def _swa_constants(dh, n_groups, seq):
    G = SWA_GROUP
    gw = G * dh
    half = dh // 2
    i = jnp.arange(gw)
    ones_bd = (i[:, None] // dh == i[None, :] // dh).astype(BF16)
    kvw = n_groups * dh
    j = jnp.arange(2 * dh)
    sel = (jnp.arange(kvw)[None, :, None] == (dh * jnp.arange(n_groups)[:, None, None] + (j % dh)[None, None, :])).astype(BF16)
    inv = ROPE_THETA ** (-jnp.arange(half, dtype=F32) / half)
    ang = jnp.arange(seq, dtype=F32)[:, None] * inv[None, :]
    cos = jnp.tile(jnp.cos(ang), (1, 2 * G))
    sin = jnp.tile(jnp.concatenate([-jnp.sin(ang), jnp.sin(ang)], axis=1), (1, G))
    return ones_bd, sel, cos, sin


def _swa(proj, q_gain, k_gain, o_gain, sinks, table, batch, seq, q_col, qw, kvw, dh):
    t = proj.shape[0]
    tab_rows = table.shape[0] // (batch * (seq // SWA_BLOCK))
    assert tab_rows * batch * (seq // SWA_BLOCK) == table.shape[0] and tab_rows % (4 * SUBLANES) == 0
    tab_spec = pl.BlockSpec((tab_rows, table.shape[1]), lambda b, s: (b * (seq // SWA_BLOCK) + s, 0))
    scale_spec = pl.BlockSpec((tab_rows, LANES), lambda b, s: (b * (seq // SWA_BLOCK) + s, 0))
    W, G = SWA_BLOCK, SWA_GROUP
    gw = G * dh
    n_groups = kvw // dh
    nb = seq // W
    assert dh & (dh - 1) == 0 and 2 * dh == LANES
    ones_bd, sel, cos, sin = _swa_constants(dh, n_groups, seq)
    gq = jnp.tile(q_gain, G)[None, :]
    gk = jnp.tile(k_gain, n_groups)[None, :]
    go = jnp.tile(o_gain, G)[None, :]
    sink_cols = jnp.repeat(sinks.reshape(-1, 2), W, axis=1)[:, :, None]
    cur = lambda b, s: b * nb + s
    prev = lambda b, s: b * nb + jnp.maximum(s - 1, 0)
    k_blk = (q_col + qw) // kvw
    const = lambda a: pl.BlockSpec(a.shape, lambda b, s: (0,) * a.ndim)
    kern = functools.partial(_swa_kernel, n_groups=n_groups, dh=dh)
    return pl.pallas_call(
        kern,
        grid=(batch, nb),
        in_specs=[pl.BlockSpec((W, qw), lambda b, s: (cur(b, s), q_col // qw)),
                  pl.BlockSpec((W, kvw), lambda b, s: (cur(b, s), k_blk)),
                  pl.BlockSpec((W, kvw), lambda b, s: (prev(b, s), k_blk)),
                  pl.BlockSpec((W, kvw), lambda b, s: (cur(b, s), k_blk + 1)),
                  pl.BlockSpec((W, kvw), lambda b, s: (prev(b, s), k_blk + 1)),
                  pl.BlockSpec((W, gw), lambda b, s: (s, 0)),
                  pl.BlockSpec((W, gw), lambda b, s: (jnp.maximum(s - 1, 0), 0)),
                  pl.BlockSpec((W, gw), lambda b, s: (s, 0)),
                  pl.BlockSpec((W, gw), lambda b, s: (jnp.maximum(s - 1, 0), 0)),
                  const(gq), const(gk), const(go), const(sink_cols), const(ones_bd), const(sel), tab_spec],
        out_specs=[pl.BlockSpec((W, qw), lambda b, s: (cur(b, s), 0)), tab_spec, scale_spec],
        out_shape=[jax.ShapeDtypeStruct((t, qw), BF16), jax.ShapeDtypeStruct(table.shape, F8),
                   jax.ShapeDtypeStruct((table.shape[0], LANES), F32)],
        compiler_params=_params("arbitrary", "arbitrary"),
        name="swa",
    )(proj, proj, proj, proj, proj, cos, cos, sin, sin, gq, gk, go, sink_cols, ones_bd, sel, table)


def _outproj_kernel(og_ref, os_ref, wg_ref, ws_ref, x_ref, g_ref, o_ref, wgb_ref, wsb_ref):
    @pl.when(pl.program_id(1) == 0)
    def _():
        wgb_ref[...] = wg_ref[...].astype(BF16)
        wsb_ref[...] = ws_ref[...].astype(BF16)

    mix = _dot(og_ref[...], wgb_ref[...]) + _dot(os_ref[...], wsb_ref[...])
    o_ref[...] = x_ref[...] + g_ref[...] * mix


def _outproj(o_gla, o_swa, w_out, x2, mod, seq, tm, tn):
    t, d = x2.shape
    half = o_gla.shape[1]
    return pl.pallas_call(
        _outproj_kernel,
        grid=(d // tn, t // tm),
        in_specs=[pl.BlockSpec((tm, half), lambda j, i: (i, 0)),
                  pl.BlockSpec((tm, half), lambda j, i: (i, 0)),
                  pl.BlockSpec((half, tn), lambda j, i: (0, j)),
                  pl.BlockSpec((half, tn), lambda j, i: (1, j)),
                  pl.BlockSpec((tm, tn), lambda j, i: (i, j)),
                  pl.BlockSpec((None, None, 1, tn), lambda j, i: (2, (i * tm) // seq, 0, j))],
        out_specs=pl.BlockSpec((tm, tn), lambda j, i: (i, j)),
        out_shape=jax.ShapeDtypeStruct((t, d), F32),
        scratch_shapes=[pltpu.VMEM((half, tn), BF16), pltpu.VMEM((half, tn), BF16)],
        compiler_params=_params("arbitrary", "arbitrary"),
        name="outproj",
    )(o_gla, o_swa, w_out, w_out, x2, mod)


def _top_values(s, k):
    vals = []
    for _ in range(k):
        m = jnp.max(s, axis=0, keepdims=True)
        vals.append(m)
        s = jnp.where(s == m, -jnp.inf, s)
    return vals


def _peerq_kernel(x_ref, sh_ref, sc_ref, w_ref, k1_ref, k2_ref, tab_ref, yt_ref, ys_ref, s1_ref, s2_ref, tc_ref,
                  tab_out_ref, y_ref, *, heads_per_step, half):
    tab_out_ref[...] = tab_ref[...].astype(BF16)

    @pl.when(pl.program_id(1) == 0)
    def _():
        y = _modulated_rms(x_ref[...], sc_ref[...], sh_ref[...])
        y_ref[...] = y.astype(BF16)
        yt_ref[...], scale = _quantize_rows(y.T, axis=0)
        ys_ref[...] = jnp.broadcast_to(scale, ys_ref.shape)

    K = PEER_TOPK
    q = _dot(y_ref[...], w_ref[...])
    tm = q.shape[0]
    for hh in range(heads_per_step):
        q1 = q[:, 2 * half * hh:2 * half * hh + half].astype(BF16)
        q2 = q[:, 2 * half * hh + half:2 * half * (hh + 1)].astype(BF16)
        s1 = _dot_nt(k1_ref[...], q1)
        s2 = _dot_nt(k2_ref[...], q2)
        s1_ref[hh] = s1
        s2_ref[hh] = s2
        v1 = _top_values(s1, K)
        v2 = jnp.concatenate(_top_values(s2, K), axis=0)
        cand = [v1[0] + v2]
        for i in range(1, K // 2):
            cand.append(v1[i] + v2[:K // 2])
        cand.append(jnp.concatenate(v1[K // 2:], axis=0) + v2[0:1])
        top = _top_values(jnp.concatenate(cand, axis=0), K)
        m = top[0]
        z = jnp.exp(top[0] - m)
        for r in range(1, K):
            z = z + jnp.exp(top[r] - m)
        tc_ref[hh] = jnp.concatenate([top[K - 1], m + jnp.log(z), jnp.zeros((SUBLANES - 2, tm), F32)], axis=0)


def _peerq(x1, mod, w_q, k1, k2, table, seq, tm, n_heads):
    t, d = x1.shape
    nq = w_q.shape[1]
    n_keys, half = k1.shape
    hps = 2
    tn = hps * 2 * half
    n_j = nq // tn
    tab_rows = table.shape[0] // ((t // tm) * n_j)
    assert tab_rows * (t // tm) * n_j == table.shape[0] and tab_rows % (2 * SUBLANES) == 0
    tab_spec = pl.BlockSpec((tab_rows, table.shape[1]), lambda i, j: (i * n_j + j, 0))
    mod_spec = lambda k: pl.BlockSpec((None, None, 1, d), lambda i, j: (k, (i * tm) // seq, 0, 0))
    kern = functools.partial(_peerq_kernel, heads_per_step=hps, half=half)
    return pl.pallas_call(
        kern,
        grid=(t // tm, n_j),
        in_specs=[pl.BlockSpec((tm, d), lambda i, j: (i, 0)),
                  mod_spec(3), mod_spec(4),
                  pl.BlockSpec((d, tn), lambda i, j: (0, j)),
                  pl.BlockSpec(k1.shape, lambda i, j: (0, 0)),
                  pl.BlockSpec(k2.shape, lambda i, j: (0, 0)),
                  tab_spec],
        out_specs=[pl.BlockSpec((d, tm), lambda i, j: (0, i)),
                   pl.BlockSpec((SUBLANES, tm), lambda i, j: (0, i)),
                   pl.BlockSpec((hps, n_keys, tm), lambda i, j: (j, 0, i)),
                   pl.BlockSpec((hps, n_keys, tm), lambda i, j: (j, 0, i)),
                   pl.BlockSpec((hps, SUBLANES, tm), lambda i, j: (j, 0, i)),
                   tab_spec],
        out_shape=[jax.ShapeDtypeStruct((d, t), F8),
                   jax.ShapeDtypeStruct((SUBLANES, t), F32),
                   jax.ShapeDtypeStruct((n_heads, n_keys, t), F32),
                   jax.ShapeDtypeStruct((n_heads, n_keys, t), F32),
                   jax.ShapeDtypeStruct((n_heads, SUBLANES, t), F32),
                   jax.ShapeDtypeStruct(table.shape, BF16)],
        scratch_shapes=[pltpu.VMEM((tm, d), BF16)],
        compiler_params=_params("arbitrary", "arbitrary"),
        name="peerq",
    )(x1, mod, mod, w_q, k1, k2, table)


def _peer_kernel(yt_ref, ys_ref, dn_ref, ds_ref, up_ref, s1_ref, s2_ref, tc_ref, x_ref, g_ref, o_ref,
                 hd_ref, gate_ref, hst_ref, *, n_heads, n_keys, te):
    s = pl.program_id(1)
    n = pl.num_programs(1) - 1
    d, tm = yt_ref.shape

    @pl.when((pl.program_id(0) == 0) & (s == 0))
    def _():
        hst_ref[...] = jnp.zeros_like(hst_ref)

    @pl.when(s == 0)
    def _():
        o_ref[...] = jnp.zeros_like(o_ref)

    per_step = te // n_keys

    @pl.when(s < n)
    def _():
        n_kc = d // MXU_TILE
        per_a = n_kc // per_step
        tw = tm // per_a
        acc = None
        for kc in range(n_kc):
            ks = slice(MXU_TILE * kc, MXU_TILE * (kc + 1))
            part = _dot(dn_ref[:, ks], yt_ref[ks, :])
            acc = part if acc is None else acc + part
            ai, q = divmod(kc, per_a)
            a = s * per_step + ai
            cols = slice(tw * q, tw * (q + 1))
            gate = None
            for h in range(n_heads):
                cand = s1_ref[h, pl.ds(a, 1), :][:, cols] + s2_ref[h, :, cols]
                term = jnp.where(cand >= tc_ref[h, 0:1, cols], jnp.exp(cand - tc_ref[h, 1:2, cols]), 0.0)
                gate = term if gate is None else gate + term
            gate_ref[n_keys * ai:n_keys * (ai + 1), cols] = gate
        hd_ref[...] = acc

    cur = s % 2
    th = tm // 2
    pieces = [(ai, half) for ai in range(per_step) for half in range(2)]
    n_nc = d // PEER_OUT_CHUNK
    done = 0
    for j in range(n_nc):
        cols = slice(PEER_OUT_CHUNK * j, PEER_OUT_CHUNK * (j + 1))
        o_ref[:, cols] += _dot(hst_ref[1 - cur], up_ref[:, cols])
        upto = (j + 1) * len(pieces) // n_nc
        for ai, half in pieces[done:upto]:
            rows = slice(n_keys * ai, n_keys * (ai + 1))
            tok = slice(th * half, th * (half + 1))
            x = hd_ref[rows, tok] * jnp.tile(ds_ref[rows, :], (1, th // LANES)) * ys_ref[0:1, tok]
            hs = 0.5 * x * (1.0 + lax.erf(x * (2.0 ** -0.5))) * gate_ref[rows, tok]
            hst_ref[cur, tok, rows] = hs.T.astype(BF16)
        done = upto

    @pl.when(s == n)
    def _():
        o_ref[...] = x_ref[...] + g_ref[...] * o_ref[...]
        hst_ref[1] = jnp.zeros_like(hst_ref[1])


def _peer(y_t, y_scale, down, down_scale, up, s1, s2, tc, x1, mod, seq, tm, te):
    d, t = y_t.shape
    n = down.shape[0] // te
    n_heads, n_keys, _ = s1.shape
    assert (d // MXU_TILE) % (te // n_keys) == 0 and d % PEER_OUT_CHUNK == 0
    once = dict(pipeline_mode=pl.Buffered(1))
    kern = functools.partial(_peer_kernel, n_heads=n_heads, n_keys=n_keys, te=te)
    return pl.pallas_call(
        kern,
        grid=(t // tm, n + 1),
        in_specs=[pl.BlockSpec((d, tm), lambda i, s: (0, i), **once),
                  pl.BlockSpec((SUBLANES, tm), lambda i, s: (0, i), **once),
                  pl.BlockSpec((te, d), lambda i, s: (jnp.minimum(s, n - 1), 0)),
                  pl.BlockSpec((te, LANES), lambda i, s: (jnp.minimum(s, n - 1), 0)),
                  pl.BlockSpec((te, d), lambda i, s: (jnp.maximum(s - 1, 0), 0)),
                  pl.BlockSpec((n_heads, n_keys, tm), lambda i, s: (0, 0, i), **once),
                  pl.BlockSpec((n_heads, n_keys, tm), lambda i, s: (0, 0, i), **once),
                  pl.BlockSpec((n_heads, SUBLANES, tm), lambda i, s: (0, 0, i), **once),
                  pl.BlockSpec((tm, d), lambda i, s: (i, 0), **once),
                  pl.BlockSpec((None, None, 1, d), lambda i, s: (5, (i * tm) // seq, 0, 0))],
        out_specs=pl.BlockSpec((tm, d), lambda i, s: (i, 0)),
        out_shape=jax.ShapeDtypeStruct((t, d), F32),
        scratch_shapes=[pltpu.VMEM((te, tm), F32), pltpu.VMEM((te, tm), F32), pltpu.VMEM((2, tm, te), BF16)],
        compiler_params=_params("arbitrary", "arbitrary"),
        name="peer",
    )(y_t, y_scale, down, down_scale, up, s1, s2, tc, x1, mod)


def _tile(total, want):
    return want if total % want == 0 else total


def kernel(x, c, w_ada, b_ada, w_in, w_gla_gate_up, b_gla_gate, gla_out_norm, swa_q_norm, swa_k_norm, swa_sinks, swa_out_norm, w_out, w_peer_q, peer_sub_keys_1, peer_sub_keys_2, peer_expert_down, peer_expert_up):
    batch, seq, d = x.shape
    t = batch * seq
    depth = w_ada.shape[0]
    kw = w_gla_gate_up.shape[2]
    dv = gla_out_norm.shape[1]
    dk = dv // 2
    gw = (kw // dk) * dv
    dh = swa_q_norm.shape[1]
    qw = swa_sinks.shape[1] * dh
    kvw = qw // SWA_GROUP
    in_width = w_in.shape[2]
    ga_old = 2 * kw + 2 * gw
    assert in_width == ga_old + GLA_GATE_RANK + qw + 2 * kvw
    assert gw == 2 * kw and qw == gw and d == gw + qw and 2 * dk == LANES and dv == LANES
    assert seq % SWA_BLOCK == 0 and kvw % LANES == 0
    n_keys, half = peer_sub_keys_1.shape[1:]
    n_heads = w_peer_q.shape[2] // (2 * half)
    assert n_keys == LANES and half == LANES and n_heads % 2 == 0

    q_col = ga_old
    n_tail = qw + 2 * kvw
    ga_col = ga_old + n_tail
    tn_in = 512 if ga_old % 512 == 0 and n_tail % 512 == 0 else 256
    assert ga_old % tn_in == 0 and n_tail % tn_in == 0 and qw >= tn_in

    tm_in = _tile(t, 1024)
    ts_gla = _tile(seq, 256)
    tm_out = _tile(seq, 1024)
    tm_peer = _tile(seq, 512)
    te = 512

    c_pad = jnp.pad(c, ((0, -batch % SUBLANES), (0, 0)))
    x2 = x.reshape(t, d)
    for l in range(depth):
        mod = _ada(c_pad, w_ada[l], b_ada[l][None, :])
        mod = mod[:batch].reshape(batch, N_MOD, d).transpose(1, 0, 2)[:, :, None, :]

        h = _modulate(x2, mod, seq, tm_peer)
        proj = _inproj(h, w_in[l].T, tm_in, tn_in, ga_old, n_tail)

        wup = jnp.pad(w_gla_gate_up[l], ((0, LANES - GLA_GATE_RANK), (0, 0))).astype(BF16)
        o_gla = _gla(proj, wup, b_gla_gate[l][None, :], gla_out_norm[l][None, :],
                     batch, seq, ts_gla, kw, gw, ga_col, dk, dv)
        o_swa, down, down_scale = _swa(proj, swa_q_norm[l], swa_k_norm[l], swa_out_norm[l], swa_sinks[l],
                           peer_expert_down[l], batch, seq, q_col, qw, kvw, dh)
        x1 = _outproj(o_gla, o_swa, w_out[l], x2, mod, seq, tm_out, 512)

        y_t, y_scale, s1, s2, tc, up = _peerq(x1, mod, w_peer_q[l].astype(BF16), peer_sub_keys_1[l].astype(BF16),
                                     peer_sub_keys_2[l].astype(BF16), peer_expert_up[l], seq, tm_peer, n_heads)
        x2 = _peer(y_t, y_scale, down, down_scale, up, s1, s2, tc, x1, mod, seq, tm_peer, te)
    return x2.reshape(batch, seq, d)
```

```python
import functools

import jax
import jax.numpy as jnp
from jax import lax
from jax.experimental import pallas as pl
from jax.experimental.pallas import tpu as pltpu

F32 = jnp.float32
BF16 = jnp.bfloat16
F8 = jnp.float8_e4m3fn
F8_MAX = 448.0
F8_AMAX_FLOOR = 2.0 ** -100

NORM_EPS = 1e-6
N_MOD = 6
GLA_GATE_RANK = 16
GLA_GATE_TEMP = 16.0
GLA_CHUNK = 64
SWA_BLOCK = 128
SWA_GROUP = 8
ROPE_THETA = 10000.0
PEER_TOPK = 16

LANES = 128
SUBLANES = 8
MXU_TILE = 256
PEER_OUT_CHUNK = 512
VMEM_LIMIT_BYTES = 60 * 1024 * 1024


def _params(*semantics):
    return pltpu.CompilerParams(dimension_semantics=semantics, vmem_limit_bytes=VMEM_LIMIT_BYTES)


def _dot(a, b):
    return jnp.dot(a, b, preferred_element_type=F32)


def _dot_nt(a, b):
    return lax.dot_general(a, b, (((1,), (1,)), ((), ())), preferred_element_type=F32)


def _dot_tn(a, b):
    return lax.dot_general(a, b, (((0,), (0,)), ((), ())), preferred_element_type=F32)


def _split(x, parts):
    out = []
    for _ in range(parts - 1):
        hi = x.astype(BF16)
        out.append(hi)
        x = x - hi.astype(F32)
    out.append(x.astype(BF16))
    return out


def _dot_split(x, m, parts):
    acc = None
    for p in _split(x, parts):
        t = _dot(p, m)
        acc = t if acc is None else acc + t
    return acc


def _quantize_rows(x, axis):
    amax = jnp.max(jnp.abs(x), axis=axis, keepdims=True)
    scale = jnp.maximum(amax, F8_AMAX_FLOOR) * (1.0 / F8_MAX)
    return (x * (1.0 / scale)).astype(F8), scale


def _pow2_scale(amax):
    bits = lax.bitcast_convert_type(jnp.maximum(amax, F8_AMAX_FLOOR), jnp.int32)
    above = lax.bitcast_convert_type((bits & 0x7F800000) + 0x00800000, F32)
    return above * (1.0 / 256.0)


def _modulated_rms(x, scale, shift):
    ms = jnp.mean(x * x, axis=-1, keepdims=True)
    return x * lax.rsqrt(ms + NORM_EPS) * (1.0 + scale) + shift


def _ada_kernel(c_ref, w_ref, b_ref, o_ref):
    c = c_ref[...]
    act = (c * jax.nn.sigmoid(c)).astype(BF16)
    o_ref[...] = _dot(act, w_ref[...].astype(BF16)) + b_ref[...]


def _ada(c_pad, w, b):
    rows, d = c_pad.shape
    n = w.shape[1]
    tn = 512
    return pl.pallas_call(
        _ada_kernel,
        grid=(n // tn,),
        in_specs=[pl.BlockSpec((rows, d), lambda j: (0, 0)),
                  pl.BlockSpec((d, tn), lambda j: (0, j)),
                  pl.BlockSpec((1, tn), lambda j: (0, j))],
        out_specs=pl.BlockSpec((rows, tn), lambda j: (0, j)),
        out_shape=jax.ShapeDtypeStruct((rows, n), F32),
        compiler_params=_params("arbitrary"),
        name="ada",
    )(c_pad, w, b)


def _modulate_kernel(x_ref, sh_ref, sc_ref, o_ref):
    o_ref[...] = _modulated_rms(x_ref[...], sc_ref[...], sh_ref[...]).astype(BF16)


def _modulate(x2, mod, seq, tm):
    t, d = x2.shape
    mod_spec = lambda k: pl.BlockSpec((None, None, 1, d), lambda i: (k, (i * tm) // seq, 0, 0))
    return pl.pallas_call(
        _modulate_kernel,
        grid=(t // tm,),
        in_specs=[pl.BlockSpec((tm, d), lambda i: (i, 0)), mod_spec(0), mod_spec(1)],
        out_specs=pl.BlockSpec((tm, d), lambda i: (i, 0)),
        out_shape=jax.ShapeDtypeStruct((t, d), BF16),
        compiler_params=_params("arbitrary"),
        name="modulate",
    )(x2, mod, mod)


def _inproj_kernel(h_ref, wt_ref, o_ref, w_ref):
    @pl.when(pl.program_id(1) == 0)
    def _():
        w_ref[...] = wt_ref[...].T.astype(BF16)

    o_ref[...] = _dot(h_ref[...], w_ref[...])


def _inproj(h, w_t, tm, tn, ga_old, n_tail):
    t, d = h.shape
    n_main = ga_old // tn
    n_sw = n_tail // tn

    def src_row(j):
        s = SUBLANES
        return s * jnp.where(j < n_main, j * (tn // s),
                             jnp.where(j < n_main + n_sw, (ga_old + GLA_GATE_RANK) // s + (j - n_main) * (tn // s),
                                       ga_old // s))

    n_tiles = n_main + n_sw + 1
    return pl.pallas_call(
        _inproj_kernel,
        grid=(n_tiles, t // tm),
        in_specs=[pl.BlockSpec((tm, d), lambda j, i: (i, 0)),
                  pl.BlockSpec((pl.Element(tn), pl.Element(d)), lambda j, i: (src_row(j), 0))],
        out_specs=pl.BlockSpec((tm, tn), lambda j, i: (i, j)),
        out_shape=jax.ShapeDtypeStruct((t, n_tiles * tn), F32),
        scratch_shapes=[pltpu.VMEM((d, tn), BF16)],
        compiler_params=_params("arbitrary", "arbitrary"),
        name="inproj",
    )(h, w_t)


def _gla_kernel(q_ref, k_ref, v_ref, g_ref, a_ref, wup_ref, bup_ref, gain_ref, tab_ref, o_ref, cmax_ref, st_ref,
                *, n_pairs, n_chunks, dk, dv):
    cmax_ref[...] = jnp.broadcast_to(jnp.max(jnp.abs(tab_ref[...]), axis=0, keepdims=True), cmax_ref.shape)

    @pl.when(pl.program_id(1) == 0)
    def _():
        st_ref[...] = jnp.zeros_like(st_ref)

    C = GLA_CHUNK
    row = lax.broadcasted_iota(jnp.int32, (C, C), 0)
    col = lax.broadcasted_iota(jnp.int32, (C, C), 1)
    tril = row >= col
    cum_op = tril.astype(BF16)
    causal2 = jnp.concatenate([tril, tril], axis=0)
    head0 = lax.broadcasted_iota(jnp.int32, (C, 2 * dk), 1) < dk
    head0_st = lax.broadcasted_iota(jnp.int32, (dv, 2 * dk), 1) < dk
    scale = dk ** -0.5
    gain = gain_ref[...]

    def stack_heads(x):
        return jnp.concatenate([jnp.where(head0, x, 0.0), jnp.where(head0, 0.0, x)], axis=0).astype(BF16)

    def chunk(c, carry):
        rows = pl.ds(pl.multiple_of(c * C, C), C)
        z = _dot(a_ref[rows, :].astype(BF16), wup_ref[...]) + bup_ref[...]
        log_a = (jnp.minimum(z, 0.0) - jnp.log1p(jnp.exp(-jnp.abs(z)))) * (1.0 / GLA_GATE_TEMP)
        b = _dot_split_left(cum_op, log_a)
        b_last = b[C - 1:C, :]
        mid = 0.5 * b_last
        qt = q_ref[rows, :] * scale * jnp.exp(b - mid)
        kt = k_ref[rows, :] * jnp.exp(mid - b)
        qi = qt * jnp.exp(mid)
        kl = (kt * jnp.exp(b_last - mid)).astype(BF16)
        kt = kt.astype(BF16)
        decay = jnp.exp(b_last)
        pair = lambda x, p: x[:, 2 * dk * p:2 * dk * (p + 1)]
        heads = range(2 * n_pairs)
        st = [st_ref[p] for p in range(n_pairs)]
        attn = [_dot_nt(stack_heads(pair(qt, p)), pair(kt, p)) for p in range(n_pairs)]
        inter = [_dot_nt(stack_heads(pair(qi, p)), st[p].astype(BF16)) for p in range(n_pairs)]
        attn = [jnp.where(causal2, a, 0.0).astype(BF16) for a in attn]
        v = [v_ref[rows, dv * h:dv * (h + 1)].astype(BF16) for h in heads]
        half = lambda x, h: x[h // 2][C * (h % 2):C * (h % 2 + 1)]
        out = [_dot(half(attn, h), v[h]) + half(inter, h) for h in heads]
        upd = [_dot_tn(v[h], pair(kl, h // 2)) for h in heads]
        for h in heads:
            o = out[h]
            on = o * lax.rsqrt(jnp.mean(o * o, axis=-1, keepdims=True) + NORM_EPS) * gain
            gg = g_ref[rows, dv * h:dv * (h + 1)]
            o_ref[rows, dv * h:dv * (h + 1)] = (on * (gg * jax.nn.sigmoid(gg))).astype(BF16)
        for p in range(n_pairs):
            st_ref[p] = st[p] * pair(decay, p) + jnp.where(head0_st, upd[2 * p], upd[2 * p + 1])
        return carry

    lax.fori_loop(0, n_chunks, chunk, 0)


def _dot_split_left(m, x):
    acc = None
    for p in _split(x, 3):
        t = _dot(m, p)
        acc = t if acc is None else acc + t
    return acc


def _gla(proj, wup, bup, gain, table, batch, seq, ts, kw, gw, ga_col, dk, dv):
    t = proj.shape[0]
    nb = seq // ts
    tab_rows = table.shape[0] // (batch * nb)
    assert tab_rows * batch * nb == table.shape[0] and tab_rows % SUBLANES == 0
    rb = lambda b, s: b * nb + s
    kern = functools.partial(_gla_kernel, n_pairs=kw // (2 * dk), n_chunks=ts // GLA_CHUNK, dk=dk, dv=dv)
    return pl.pallas_call(
        kern,
        grid=(batch, nb),
        in_specs=[pl.BlockSpec((ts, kw), lambda b, s: (rb(b, s), 0)),
                  pl.BlockSpec((ts, kw), lambda b, s: (rb(b, s), 1)),
                  pl.BlockSpec((ts, gw), lambda b, s: (rb(b, s), (2 * kw) // gw)),
                  pl.BlockSpec((ts, gw), lambda b, s: (rb(b, s), (2 * kw) // gw + 1)),
                  pl.BlockSpec((ts, LANES), lambda b, s: (rb(b, s), ga_col // LANES)),
                  pl.BlockSpec(wup.shape, lambda b, s: (0, 0)),
                  pl.BlockSpec(bup.shape, lambda b, s: (0, 0)),
                  pl.BlockSpec(gain.shape, lambda b, s: (0, 0)),
                  pl.BlockSpec((tab_rows, table.shape[1]), lambda b, s: (rb(b, s), 0))],
        out_specs=[pl.BlockSpec((ts, gw), lambda b, s: (rb(b, s), 0)),
                   pl.BlockSpec((None, SUBLANES, table.shape[1]), lambda b, s: (rb(b, s), 0, 0))],
        out_shape=[jax.ShapeDtypeStruct((t, gw), BF16),
                   jax.ShapeDtypeStruct((batch * nb, SUBLANES, table.shape[1]), F32)],
        scratch_shapes=[pltpu.VMEM((kw // (2 * dk), dv, 2 * dk), F32)],
        compiler_params=_params("arbitrary", "arbitrary"),
        name="gla",
    )(proj, proj, proj, proj, proj, wup, bup, gain, table)


def _group_rms(x, ones_bd, dh):
    ss = _dot_split(x * x, ones_bd, 2)
    return x * lax.rsqrt(ss * (1.0 / dh) + NORM_EPS)


def _rope(x, cos, sin_signed, dh):
    half = dh // 2
    width = x.shape[1]
    first_half = (lax.broadcasted_iota(jnp.int32, x.shape, 1) & (dh - 1)) < half
    partner = jnp.where(first_half, pltpu.roll(x, width - half, 1), pltpu.roll(x, half, 1))
    return x * cos + partner * sin_signed


def _swa_kernel(q_ref, kc_ref, kp_ref, vc_ref, vp_ref, cc_ref, cp_ref, sc_ref, sp_ref,
                gq_ref, gk_ref, go_ref, sink_ref, ones_ref, sel_ref, tab_ref, o_ref, tab_out_ref, tab_scale_ref,
                *, n_groups, dh):
    W, G = SWA_BLOCK, SWA_GROUP
    n = pl.program_id(1)
    gw = G * dh
    kvw = n_groups * dh
    pairs = G // 2
    tab_out_ref[...], scale = _quantize_rows(tab_ref[...], axis=1)
    tab_scale_ref[...] = jnp.broadcast_to(scale, tab_scale_ref.shape)

    k_all = jnp.concatenate([kp_ref[...], kc_ref[...]], axis=0)
    cos_k = jnp.concatenate([cp_ref[:, :kvw], cc_ref[:, :kvw]], axis=0)
    sin_k = jnp.concatenate([sp_ref[:, :kvw], sc_ref[:, :kvw]], axis=0)
    k_r = _rope(_group_rms(k_all, ones_ref[:kvw, :kvw], dh) * gk_ref[...], cos_k, sin_k, dh)
    k_t = k_r.T.astype(BF16)
    v_all = jnp.concatenate([vp_ref[...], vc_ref[...]], axis=0).astype(BF16)

    qi = lax.broadcasted_iota(jnp.int32, (2 * W, 2 * W), 0) & (W - 1)
    kj = lax.broadcasted_iota(jnp.int32, (2 * W, 2 * W), 1)
    rel = kj - qi - 1
    first_key = jnp.where(n > 0, 0, W)
    valid = (rel >= 0) & (rel < W) & (kj >= first_key)
    head0 = lax.broadcasted_iota(jnp.int32, (W, 2 * dh), 1) < dh
    ones_cols = jnp.ones((2 * W, 2 * dh), BF16)

    for g in range(n_groups):
        q_n = _group_rms(q_ref[:, gw * g:gw * (g + 1)], ones_ref[...], dh) * gq_ref[...]
        q_r = _rope(q_n, cc_ref[...], sc_ref[...], dh)
        k_tg = k_t[dh * g:dh * (g + 1), :]
        k_t2 = jnp.concatenate([k_tg, k_tg], axis=0)
        v3 = jnp.concatenate([_dot(v_all, sel_ref[g]).astype(BF16), ones_cols], axis=1)
        outs = []
        for pp in range(pairs):
            qp = q_r[:, 2 * dh * pp:2 * dh * (pp + 1)]
            q2 = jnp.concatenate([jnp.where(head0, qp, 0.0), jnp.where(head0, 0.0, qp)], axis=0).astype(BF16)
            s = jnp.where(valid, _dot(q2, k_t2) * (dh ** -0.5), -jnp.inf)
            sink = sink_ref[pairs * g + pp]
            m = jnp.maximum(jnp.max(s, axis=-1, keepdims=True), sink)
            r = _dot(jnp.exp(s - m).astype(BF16), v3)
            o2 = r[:, :2 * dh] / (r[:, 2 * dh:] + jnp.exp(sink - m))
            outs.append(jnp.where(head0, o2[:W], o2[W:]))
        o = jnp.concatenate(outs, axis=1)
        o_ref[:, gw * g:gw * (g + 1)] = (_group_rms(o, ones_ref[...], dh) * go_ref[...]).astype(BF16)


def _swa_constants(dh, n_groups, seq):
    G = SWA_GROUP
    gw = G * dh
    half = dh // 2
    i = jnp.arange(gw)
    ones_bd = (i[:, None] // dh == i[None, :] // dh).astype(BF16)
    kvw = n_groups * dh
    j = jnp.arange(2 * dh)
    sel = (jnp.arange(kvw)[None, :, None] == (dh * jnp.arange(n_groups)[:, None, None] + (j % dh)[None, None, :])).astype(BF16)
    inv = ROPE_THETA ** (-jnp.arange(half, dtype=F32) / half)
    ang = jnp.arange(seq, dtype=F32)[:, None] * inv[None, :]
    cos = jnp.tile(jnp.cos(ang), (1, 2 * G))
    sin = jnp.tile(jnp.concatenate([-jnp.sin(ang), jnp.sin(ang)], axis=1), (1, G))
    return ones_bd, sel, cos, sin


def _swa(proj, q_gain, k_gain, o_gain, sinks, table, batch, seq, q_col, qw, kvw, dh):
    t = proj.shape[0]
    tab_rows = table.shape[0] // (batch * (seq // SWA_BLOCK))
    assert tab_rows * batch * (seq // SWA_BLOCK) == table.shape[0] and tab_rows % (4 * SUBLANES) == 0
    tab_spec = pl.BlockSpec((tab_rows, table.shape[1]), lambda b, s: (b * (seq // SWA_BLOCK) + s, 0))
    scale_spec = pl.BlockSpec((tab_rows, LANES), lambda b, s: (b * (seq // SWA_BLOCK) + s, 0))
    W, G = SWA_BLOCK, SWA_GROUP
    gw = G * dh
    n_groups = kvw // dh
    nb = seq // W
    assert dh & (dh - 1) == 0 and 2 * dh == LANES
    ones_bd, sel, cos, sin = _swa_constants(dh, n_groups, seq)
    gq = jnp.tile(q_gain, G)[None, :]
    gk = jnp.tile(k_gain, n_groups)[None, :]
    go = jnp.tile(o_gain, G)[None, :]
    sink_cols = jnp.repeat(sinks.reshape(-1, 2), W, axis=1)[:, :, None]
    cur = lambda b, s: b * nb + s
    prev = lambda b, s: b * nb + jnp.maximum(s - 1, 0)
    k_blk = (q_col + qw) // kvw
    const = lambda a: pl.BlockSpec(a.shape, lambda b, s: (0,) * a.ndim)
    kern = functools.partial(_swa_kernel, n_groups=n_groups, dh=dh)
    return pl.pallas_call(
        kern,
        grid=(batch, nb),
        in_specs=[pl.BlockSpec((W, qw), lambda b, s: (cur(b, s), q_col // qw)),
                  pl.BlockSpec((W, kvw), lambda b, s: (cur(b, s), k_blk)),
                  pl.BlockSpec((W, kvw), lambda b, s: (prev(b, s), k_blk)),
                  pl.BlockSpec((W, kvw), lambda b, s: (cur(b, s), k_blk + 1)),
                  pl.BlockSpec((W, kvw), lambda b, s: (prev(b, s), k_blk + 1)),
                  pl.BlockSpec((W, gw), lambda b, s: (s, 0)),
                  pl.BlockSpec((W, gw), lambda b, s: (jnp.maximum(s - 1, 0), 0)),
                  pl.BlockSpec((W, gw), lambda b, s: (s, 0)),
                  pl.BlockSpec((W, gw), lambda b, s: (jnp.maximum(s - 1, 0), 0)),
                  const(gq), const(gk), const(go), const(sink_cols), const(ones_bd), const(sel), tab_spec],
        out_specs=[pl.BlockSpec((W, qw), lambda b, s: (cur(b, s), 0)), tab_spec, scale_spec],
        out_shape=[jax.ShapeDtypeStruct((t, qw), BF16), jax.ShapeDtypeStruct(table.shape, F8),
                   jax.ShapeDtypeStruct((table.shape[0], LANES), F32)],
        compiler_params=_params("arbitrary", "arbitrary"),
        name="swa",
    )(proj, proj, proj, proj, proj, cos, cos, sin, sin, gq, gk, go, sink_cols, ones_bd, sel, table)


def _outproj_kernel(og_ref, os_ref, wg_ref, ws_ref, x_ref, g_ref, o_ref, wgb_ref, wsb_ref):
    @pl.when(pl.program_id(1) == 0)
    def _():
        wgb_ref[...] = wg_ref[...].astype(BF16)
        wsb_ref[...] = ws_ref[...].astype(BF16)

    mix = _dot(og_ref[...], wgb_ref[...]) + _dot(os_ref[...], wsb_ref[...])
    o_ref[...] = x_ref[...] + g_ref[...] * mix


def _outproj(o_gla, o_swa, w_out, x2, mod, seq, tm, tn):
    t, d = x2.shape
    half = o_gla.shape[1]
    return pl.pallas_call(
        _outproj_kernel,
        grid=(d // tn, t // tm),
        in_specs=[pl.BlockSpec((tm, half), lambda j, i: (i, 0)),
                  pl.BlockSpec((tm, half), lambda j, i: (i, 0)),
                  pl.BlockSpec((half, tn), lambda j, i: (0, j)),
                  pl.BlockSpec((half, tn), lambda j, i: (1, j)),
                  pl.BlockSpec((tm, tn), lambda j, i: (i, j)),
                  pl.BlockSpec((None, None, 1, tn), lambda j, i: (2, (i * tm) // seq, 0, j))],
        out_specs=pl.BlockSpec((tm, tn), lambda j, i: (i, j)),
        out_shape=jax.ShapeDtypeStruct((t, d), F32),
        scratch_shapes=[pltpu.VMEM((half, tn), BF16), pltpu.VMEM((half, tn), BF16)],
        compiler_params=_params("arbitrary", "arbitrary"),
        name="outproj",
    )(o_gla, o_swa, w_out, w_out, x2, mod)


def _top_values(s, k):
    vals = []
    for _ in range(k):
        m = jnp.max(s, axis=0, keepdims=True)
        vals.append(m)
        s = jnp.where(s == m, -jnp.inf, s)
    return vals


def _peerq_kernel(x_ref, sh_ref, sc_ref, w_ref, k1_ref, k2_ref, tab_ref, tinv_ref, yt_ref, ys_ref, s1_ref, s2_ref,
                  tc_ref, tab_out_ref, y_ref, *, heads_per_step, half):
    tab_out_ref[...] = (tab_ref[...] * tinv_ref[...]).astype(F8)

    @pl.when(pl.program_id(1) == 0)
    def _():
        y = _modulated_rms(x_ref[...], sc_ref[...], sh_ref[...])
        y_ref[...] = y.astype(BF16)
        yt_ref[...], scale = _quantize_rows(y.T, axis=0)
        ys_ref[...] = jnp.broadcast_to(scale, ys_ref.shape)

    K = PEER_TOPK
    q = _dot(y_ref[...], w_ref[...])
    tm = q.shape[0]
    for hh in range(heads_per_step):
        q1 = q[:, 2 * half * hh:2 * half * hh + half].astype(BF16)
        q2 = q[:, 2 * half * hh + half:2 * half * (hh + 1)].astype(BF16)
        s1 = _dot_nt(k1_ref[...], q1)
        s2 = _dot_nt(k2_ref[...], q2)
        s1_ref[hh] = s1
        s2_ref[hh] = s2
        v1 = _top_values(s1, K)
        v2 = jnp.concatenate(_top_values(s2, K), axis=0)
        cand = [v1[0] + v2]
        for i in range(1, K // 2):
            cand.append(v1[i] + v2[:K // 2])
        cand.append(jnp.concatenate(v1[K // 2:], axis=0) + v2[0:1])
        top = _top_values(jnp.concatenate(cand, axis=0), K)
        m = top[0]
        z = jnp.exp(top[0] - m)
        for r in range(1, K):
            z = z + jnp.exp(top[r] - m)
        tc_ref[hh] = jnp.concatenate([top[K - 1], m + jnp.log(z), jnp.zeros((SUBLANES - 2, tm), F32)], axis=0)


def _peerq(x1, mod, w_q, k1, k2, table, table_inv_scale, seq, tm, n_heads):
    t, d = x1.shape
    nq = w_q.shape[1]
    n_keys, half = k1.shape
    hps = 2
    tn = hps * 2 * half
    n_j = nq // tn
    tab_rows = table.shape[0] // ((t // tm) * n_j)
    assert tab_rows * (t // tm) * n_j == table.shape[0] and tab_rows % (4 * SUBLANES) == 0
    tab_spec = pl.BlockSpec((tab_rows, table.shape[1]), lambda i, j: (i * n_j + j, 0))
    mod_spec = lambda k: pl.BlockSpec((None, None, 1, d), lambda i, j: (k, (i * tm) // seq, 0, 0))
    kern = functools.partial(_peerq_kernel, heads_per_step=hps, half=half)
    return pl.pallas_call(
        kern,
        grid=(t // tm, n_j),
        in_specs=[pl.BlockSpec((tm, d), lambda i, j: (i, 0)),
                  mod_spec(3), mod_spec(4),
                  pl.BlockSpec((d, tn), lambda i, j: (0, j)),
                  pl.BlockSpec(k1.shape, lambda i, j: (0, 0)),
                  pl.BlockSpec(k2.shape, lambda i, j: (0, 0)),
                  tab_spec,
                  pl.BlockSpec(table_inv_scale.shape, lambda i, j: (0, 0))],
        out_specs=[pl.BlockSpec((d, tm), lambda i, j: (0, i)),
                   pl.BlockSpec((SUBLANES, tm), lambda i, j: (0, i)),
                   pl.BlockSpec((hps, n_keys, tm), lambda i, j: (j, 0, i)),
                   pl.BlockSpec((hps, n_keys, tm), lambda i, j: (j, 0, i)),
                   pl.BlockSpec((hps, SUBLANES, tm), lambda i, j: (j, 0, i)),
                   tab_spec],
        out_shape=[jax.ShapeDtypeStruct((d, t), F8),
                   jax.ShapeDtypeStruct((SUBLANES, t), F32),
                   jax.ShapeDtypeStruct((n_heads, n_keys, t), F32),
                   jax.ShapeDtypeStruct((n_heads, n_keys, t), F32),
                   jax.ShapeDtypeStruct((n_heads, SUBLANES, t), F32),
                   jax.ShapeDtypeStruct(table.shape, F8)],
        scratch_shapes=[pltpu.VMEM((tm, d), BF16)],
        compiler_params=_params("arbitrary", "arbitrary"),
        name="peerq",
    )(x1, mod, mod, w_q, k1, k2, table, table_inv_scale)


def _peer_kernel(yt_ref, ys_ref, dn_ref, ds_ref, up_ref, s1_ref, s2_ref, tc_ref, x_ref, g_ref, o_ref,
                 hd_ref, gate_ref, hst_ref, st_ref, *, n_heads, n_keys, te):
    s = pl.program_id(1)
    n = pl.num_programs(1) - 1
    d, tm = yt_ref.shape

    @pl.when((pl.program_id(0) == 0) & (s == 0))
    def _():
        hst_ref[...] = jnp.zeros_like(hst_ref)
        st_ref[...] = jnp.zeros_like(st_ref)

    @pl.when(s == 0)
    def _():
        o_ref[...] = jnp.zeros_like(o_ref)

    per_step = te // n_keys

    @pl.when(s < n)
    def _():
        n_kc = d // MXU_TILE
        per_a = n_kc // per_step
        tw = tm // per_a
        acc = None
        for kc in range(n_kc):
            ks = slice(MXU_TILE * kc, MXU_TILE * (kc + 1))
            part = _dot(dn_ref[:, ks], yt_ref[ks, :])
            acc = part if acc is None else acc + part
            ai, q = divmod(kc, per_a)
            a = s * per_step + ai
            cols = slice(tw * q, tw * (q + 1))
            gate = None
            for h in range(n_heads):
                cand = s1_ref[h, pl.ds(a, 1), :][:, cols] + s2_ref[h, :, cols]
                term = jnp.where(cand >= tc_ref[h, 0:1, cols], jnp.exp(cand - tc_ref[h, 1:2, cols]), 0.0)
                gate = term if gate is None else gate + term
            gate_ref[n_keys * ai:n_keys * (ai + 1), cols] = gate
        hd_ref[...] = acc

    cur = s % 2
    th = tm // 2
    tok_scale = _dot_tn(st_ref[1 - cur].astype(BF16), jnp.ones((SUBLANES, LANES), BF16))
    tok_scale = jnp.tile(tok_scale, (1, PEER_OUT_CHUNK // LANES))
    for j in range(d // PEER_OUT_CHUNK):
        cols = slice(PEER_OUT_CHUNK * j, PEER_OUT_CHUNK * (j + 1))
        o_ref[:, cols] += _dot(hst_ref[1 - cur], up_ref[:, cols]) * tok_scale

    pieces = [(slice(n_keys * ai, n_keys * (ai + 1)), slice(th * half, th * (half + 1)), half)
              for ai in range(per_step) for half in range(2)]
    amax = [None, None]
    for rows, tok, half in pieces:
        x = hd_ref[rows, tok] * jnp.tile(ds_ref[rows, :], (1, th // LANES)) * ys_ref[0:1, tok]
        hs = 0.5 * x * (1.0 + lax.erf(x * (2.0 ** -0.5))) * gate_ref[rows, tok]
        hd_ref[rows, tok] = hs
        m = jnp.max(jnp.abs(hs), axis=0, keepdims=True)
        amax[half] = m if amax[half] is None else jnp.maximum(amax[half], m)
    scale = _pow2_scale(jnp.concatenate(amax, axis=1))
    inv = 1.0 / scale
    for rows, tok, half in pieces:
        hst_ref[cur, tok, rows] = (hd_ref[rows, tok] * inv[:, tok]).T.astype(F8)
    st_ref[cur] = jnp.concatenate([scale, jnp.zeros((SUBLANES - 1, tm), F32)], axis=0)

    @pl.when(s == n)
    def _():
        o_ref[...] = x_ref[...] + g_ref[...] * o_ref[...]
        hst_ref[1] = jnp.zeros_like(hst_ref[1])


def _peer(y_t, y_scale, down, down_scale, up, s1, s2, tc, x1, gate_scale, seq, tm, te):
    d, t = y_t.shape
    n = down.shape[0] // te
    n_heads, n_keys, _ = s1.shape
    assert (d // MXU_TILE) % (te // n_keys) == 0 and d % PEER_OUT_CHUNK == 0
    once = dict(pipeline_mode=pl.Buffered(1))
    kern = functools.partial(_peer_kernel, n_heads=n_heads, n_keys=n_keys, te=te)
    return pl.pallas_call(
        kern,
        grid=(t // tm, n + 1),
        in_specs=[pl.BlockSpec((d, tm), lambda i, s: (0, i), **once),
                  pl.BlockSpec((SUBLANES, tm), lambda i, s: (0, i), **once),
                  pl.BlockSpec((te, d), lambda i, s: (jnp.minimum(s, n - 1), 0)),
                  pl.BlockSpec((te, LANES), lambda i, s: (jnp.minimum(s, n - 1), 0)),
                  pl.BlockSpec((te, d), lambda i, s: (jnp.maximum(s - 1, 0), 0)),
                  pl.BlockSpec((n_heads, n_keys, tm), lambda i, s: (0, 0, i), **once),
                  pl.BlockSpec((n_heads, n_keys, tm), lambda i, s: (0, 0, i), **once),
                  pl.BlockSpec((n_heads, SUBLANES, tm), lambda i, s: (0, 0, i), **once),
                  pl.BlockSpec((tm, d), lambda i, s: (i, 0), **once),
                  pl.BlockSpec((None, None, 1, d), lambda i, s: (0, (i * tm) // seq, 0, 0))],
        out_specs=pl.BlockSpec((tm, d), lambda i, s: (i, 0)),
        out_shape=jax.ShapeDtypeStruct((t, d), F32),
        scratch_shapes=[pltpu.VMEM((te, tm), F32), pltpu.VMEM((te, tm), F32), pltpu.VMEM((2, tm, te), F8),
                        pltpu.VMEM((2, SUBLANES, tm), F32)],
        compiler_params=_params("arbitrary", "arbitrary"),
        name="peer",
    )(y_t, y_scale, down, down_scale, up, s1, s2, tc, x1, gate_scale)


def _tile(total, want):
    return want if total % want == 0 else total


def kernel(x, c, w_ada, b_ada, w_in, w_gla_gate_up, b_gla_gate, gla_out_norm, swa_q_norm, swa_k_norm, swa_sinks, swa_out_norm, w_out, w_peer_q, peer_sub_keys_1, peer_sub_keys_2, peer_expert_down, peer_expert_up):
    batch, seq, d = x.shape
    t = batch * seq
    depth = w_ada.shape[0]
    kw = w_gla_gate_up.shape[2]
    dv = gla_out_norm.shape[1]
    dk = dv // 2
    gw = (kw // dk) * dv
    dh = swa_q_norm.shape[1]
    qw = swa_sinks.shape[1] * dh
    kvw = qw // SWA_GROUP
    in_width = w_in.shape[2]
    ga_old = 2 * kw + 2 * gw
    assert in_width == ga_old + GLA_GATE_RANK + qw + 2 * kvw
    assert gw == 2 * kw and qw == gw and d == gw + qw and 2 * dk == LANES and dv == LANES
    assert seq % SWA_BLOCK == 0 and kvw % LANES == 0
    n_keys, half = peer_sub_keys_1.shape[1:]
    n_heads = w_peer_q.shape[2] // (2 * half)
    assert n_keys == LANES and half == LANES and n_heads % 2 == 0

    q_col = ga_old
    n_tail = qw + 2 * kvw
    ga_col = ga_old + n_tail
    tn_in = 512 if ga_old % 512 == 0 and n_tail % 512 == 0 else 256
    assert ga_old % tn_in == 0 and n_tail % tn_in == 0 and qw >= tn_in

    tm_in = _tile(t, 1024)
    ts_gla = _tile(seq, 256)
    tm_out = _tile(seq, 1024)
    tm_peer = _tile(seq, 512)
    te = 512

    c_pad = jnp.pad(c, ((0, -batch % SUBLANES), (0, 0)))
    x2 = x.reshape(t, d)
    for l in range(depth):
        mod = _ada(c_pad, w_ada[l], b_ada[l][None, :])
        mod = mod[:batch].reshape(batch, N_MOD, d).transpose(1, 0, 2)[:, :, None, :]

        h = _modulate(x2, mod, seq, tm_peer)
        proj = _inproj(h, w_in[l].T, tm_in, tn_in, ga_old, n_tail)

        wup = jnp.pad(w_gla_gate_up[l], ((0, LANES - GLA_GATE_RANK), (0, 0))).astype(BF16)
        o_gla, up_cmax = _gla(proj, wup, b_gla_gate[l][None, :], gla_out_norm[l][None, :], peer_expert_up[l],
                              batch, seq, ts_gla, kw, gw, ga_col, dk, dv)
        up_scale = jnp.maximum(jnp.max(up_cmax, axis=(0, 1)), F8_AMAX_FLOOR)[None, :] * (1.0 / F8_MAX)
        o_swa, down, down_scale = _swa(proj, swa_q_norm[l], swa_k_norm[l], swa_out_norm[l], swa_sinks[l],
                           peer_expert_down[l], batch, seq, q_col, qw, kvw, dh)
        x1 = _outproj(o_gla, o_swa, w_out[l], x2, mod, seq, tm_out, 512)

        y_t, y_scale, s1, s2, tc, up = _peerq(x1, mod, w_peer_q[l].astype(BF16), peer_sub_keys_1[l].astype(BF16),
                                     peer_sub_keys_2[l].astype(BF16), peer_expert_up[l], 1.0 / up_scale,
                                     seq, tm_peer, n_heads)
        x2 = _peer(y_t, y_scale, down, down_scale, up, s1, s2, tc, x1, (mod[5] * up_scale)[None],
                   seq, tm_peer, te)
    return x2.reshape(batch, seq, d)
```

```python
import functools

import jax
import jax.numpy as jnp
from jax import lax
from jax.experimental import pallas as pl
from jax.experimental.pallas import tpu as pltpu

F32 = jnp.float32
BF16 = jnp.bfloat16
F8 = jnp.float8_e4m3fn
F8_MAX = 448.0
F8_AMAX_FLOOR = 2.0 ** -100

NORM_EPS = 1e-6
N_MOD = 6
GLA_GATE_RANK = 16
GLA_GATE_TEMP = 16.0
GLA_CHUNK = 64
SWA_BLOCK = 128
SWA_GROUP = 8
ROPE_THETA = 10000.0
PEER_TOPK = 16

LANES = 128
SUBLANES = 8
MXU_TILE = 256
PEER_OUT_CHUNK = 512
VMEM_LIMIT_BYTES = 60 * 1024 * 1024


def _params(*semantics):
    return pltpu.CompilerParams(dimension_semantics=semantics, vmem_limit_bytes=VMEM_LIMIT_BYTES)


def _dot(a, b):
    return jnp.dot(a, b, preferred_element_type=F32)


def _dot_nt(a, b):
    return lax.dot_general(a, b, (((1,), (1,)), ((), ())), preferred_element_type=F32)


def _dot_tn(a, b):
    return lax.dot_general(a, b, (((0,), (0,)), ((), ())), preferred_element_type=F32)


def _split(x, parts):
    out = []
    for _ in range(parts - 1):
        hi = x.astype(BF16)
        out.append(hi)
        x = x - hi.astype(F32)
    out.append(x.astype(BF16))
    return out


def _dot_split(x, m, parts):
    acc = None
    for p in _split(x, parts):
        t = _dot(p, m)
        acc = t if acc is None else acc + t
    return acc


def _quantize_rows(x, axis):
    amax = jnp.max(jnp.abs(x), axis=axis, keepdims=True)
    scale = jnp.maximum(amax, F8_AMAX_FLOOR) * (1.0 / F8_MAX)
    return (x * (1.0 / scale)).astype(F8), scale


def _pow2_scale(amax):
    bits = lax.bitcast_convert_type(jnp.maximum(amax, F8_AMAX_FLOOR), jnp.int32)
    above = lax.bitcast_convert_type((bits & 0x7F800000) + 0x00800000, F32)
    return above * (1.0 / 256.0)


def _modulated_rms(x, scale, shift):
    ms = jnp.mean(x * x, axis=-1, keepdims=True)
    return x * lax.rsqrt(ms + NORM_EPS) * (1.0 + scale) + shift


def _ada_kernel(c_ref, w_ref, b_ref, o_ref):
    c = c_ref[...]
    act = (c * jax.nn.sigmoid(c)).astype(BF16)
    o_ref[...] = _dot(act, w_ref[...].astype(BF16)) + b_ref[...]


def _ada(c_pad, w, b):
    rows, d = c_pad.shape
    n = w.shape[1]
    tn = 512
    return pl.pallas_call(
        _ada_kernel,
        grid=(n // tn,),
        in_specs=[pl.BlockSpec((rows, d), lambda j: (0, 0)),
                  pl.BlockSpec((d, tn), lambda j: (0, j)),
                  pl.BlockSpec((1, tn), lambda j: (0, j))],
        out_specs=pl.BlockSpec((rows, tn), lambda j: (0, j)),
        out_shape=jax.ShapeDtypeStruct((rows, n), F32),
        compiler_params=_params("arbitrary"),
        name="ada",
    )(c_pad, w, b)


def _modulate_kernel(x_ref, sh_ref, sc_ref, o_ref):
    o_ref[...] = _modulated_rms(x_ref[...], sc_ref[...], sh_ref[...]).astype(BF16)


def _modulate(x2, mod, seq, tm):
    t, d = x2.shape
    mod_spec = lambda k: pl.BlockSpec((None, None, 1, d), lambda i: (k, (i * tm) // seq, 0, 0))
    return pl.pallas_call(
        _modulate_kernel,
        grid=(t // tm,),
        in_specs=[pl.BlockSpec((tm, d), lambda i: (i, 0)), mod_spec(0), mod_spec(1)],
        out_specs=pl.BlockSpec((tm, d), lambda i: (i, 0)),
        out_shape=jax.ShapeDtypeStruct((t, d), BF16),
        compiler_params=_params("arbitrary"),
        name="modulate",
    )(x2, mod, mod)


def _inproj_kernel(h_ref, wt_ref, o_ref, w_ref):
    @pl.when(pl.program_id(1) == 0)
    def _():
        w_ref[...] = wt_ref[...].T.astype(BF16)

    o_ref[...] = _dot(h_ref[...], w_ref[...])


def _inproj(h, w_t, tm, tn, ga_old, n_tail):
    t, d = h.shape
    n_main = ga_old // tn
    n_sw = n_tail // tn

    def src_row(j):
        s = SUBLANES
        return s * jnp.where(j < n_main, j * (tn // s),
                             jnp.where(j < n_main + n_sw, (ga_old + GLA_GATE_RANK) // s + (j - n_main) * (tn // s),
                                       ga_old // s))

    n_tiles = n_main + n_sw + 1
    return pl.pallas_call(
        _inproj_kernel,
        grid=(n_tiles, t // tm),
        in_specs=[pl.BlockSpec((tm, d), lambda j, i: (i, 0)),
                  pl.BlockSpec((pl.Element(tn), pl.Element(d)), lambda j, i: (src_row(j), 0))],
        out_specs=pl.BlockSpec((tm, tn), lambda j, i: (i, j)),
        out_shape=jax.ShapeDtypeStruct((t, n_tiles * tn), F32),
        scratch_shapes=[pltpu.VMEM((d, tn), BF16)],
        compiler_params=_params("arbitrary", "arbitrary"),
        name="inproj",
    )(h, w_t)


def _gla_kernel(q_ref, k_ref, v_ref, g_ref, a_ref, wup_ref, bup_ref, gain_ref, tab_ref, o_ref, cmax_ref, st_ref,
                *, n_pairs, n_chunks, dk, dv):
    cmax_ref[...] = jnp.broadcast_to(jnp.max(jnp.abs(tab_ref[...]), axis=0, keepdims=True), cmax_ref.shape)

    @pl.when(pl.program_id(1) == 0)
    def _():
        st_ref[...] = jnp.zeros_like(st_ref)

    C = GLA_CHUNK
    row = lax.broadcasted_iota(jnp.int32, (C, C), 0)
    col = lax.broadcasted_iota(jnp.int32, (C, C), 1)
    tril = row >= col
    cum_op = tril.astype(BF16)
    causal2 = jnp.concatenate([tril, tril], axis=0)
    head0 = lax.broadcasted_iota(jnp.int32, (C, 2 * dk), 1) < dk
    head0_st = lax.broadcasted_iota(jnp.int32, (dv, 2 * dk), 1) < dk
    scale = dk ** -0.5
    gain = gain_ref[...]

    def stack_heads(x):
        return jnp.concatenate([jnp.where(head0, x, 0.0), jnp.where(head0, 0.0, x)], axis=0).astype(BF16)

    def chunk(c, carry):
        rows = pl.ds(pl.multiple_of(c * C, C), C)
        z = _dot(a_ref[rows, :].astype(BF16), wup_ref[...]) + bup_ref[...]
        log_a = (jnp.minimum(z, 0.0) - jnp.log1p(jnp.exp(-jnp.abs(z)))) * (1.0 / GLA_GATE_TEMP)
        b = _dot_split_left(cum_op, log_a)
        b_last = b[C - 1:C, :]
        mid = 0.5 * b_last
        qt = q_ref[rows, :] * scale * jnp.exp(b - mid)
        kt = k_ref[rows, :] * jnp.exp(mid - b)
        qi = qt * jnp.exp(mid)
        kl = (kt * jnp.exp(b_last - mid)).astype(BF16)
        kt = kt.astype(BF16)
        decay = jnp.exp(b_last)
        pair = lambda x, p: x[:, 2 * dk * p:2 * dk * (p + 1)]
        heads = range(2 * n_pairs)
        st = [st_ref[p] for p in range(n_pairs)]
        attn = [_dot_nt(stack_heads(pair(qt, p)), pair(kt, p)) for p in range(n_pairs)]
        inter = [_dot_nt(stack_heads(pair(qi, p)), st[p].astype(BF16)) for p in range(n_pairs)]
        attn = [jnp.where(causal2, a, 0.0).astype(BF16) for a in attn]
        v = [v_ref[rows, dv * h:dv * (h + 1)].astype(BF16) for h in heads]
        half = lambda x, h: x[h // 2][C * (h % 2):C * (h % 2 + 1)]
        out = [_dot(half(attn, h), v[h]) + half(inter, h) for h in heads]
        upd = [_dot_tn(v[h], pair(kl, h // 2)) for h in heads]
        for h in heads:
            o = out[h]
            on = o * lax.rsqrt(jnp.mean(o * o, axis=-1, keepdims=True) + NORM_EPS) * gain
            gg = g_ref[rows, dv * h:dv * (h + 1)]
            o_ref[rows, dv * h:dv * (h + 1)] = (on * (gg * jax.nn.sigmoid(gg))).astype(BF16)
        for p in range(n_pairs):
            st_ref[p] = st[p] * pair(decay, p) + jnp.where(head0_st, upd[2 * p], upd[2 * p + 1])
        return carry

    lax.fori_loop(0, n_chunks, chunk, 0)


def _dot_split_left(m, x):
    acc = None
    for p in _split(x, 3):
        t = _dot(m, p)
        acc = t if acc is None else acc + t
    return acc


def _gla(proj, wup, bup, gain, table, batch, seq, ts, kw, gw, ga_col, dk, dv):
    t = proj.shape[0]
    nb = seq // ts
    tab_rows = table.shape[0] // (batch * nb)
    assert tab_rows * batch * nb == table.shape[0] and tab_rows % SUBLANES == 0
    rb = lambda b, s: b * nb + s
    kern = functools.partial(_gla_kernel, n_pairs=kw // (2 * dk), n_chunks=ts // GLA_CHUNK, dk=dk, dv=dv)
    return pl.pallas_call(
        kern,
        grid=(batch, nb),
        in_specs=[pl.BlockSpec((ts, kw), lambda b, s: (rb(b, s), 0)),
                  pl.BlockSpec((ts, kw), lambda b, s: (rb(b, s), 1)),
                  pl.BlockSpec((ts, gw), lambda b, s: (rb(b, s), (2 * kw) // gw)),
                  pl.BlockSpec((ts, gw), lambda b, s: (rb(b, s), (2 * kw) // gw + 1)),
                  pl.BlockSpec((ts, LANES), lambda b, s: (rb(b, s), ga_col // LANES)),
                  pl.BlockSpec(wup.shape, lambda b, s: (0, 0)),
                  pl.BlockSpec(bup.shape, lambda b, s: (0, 0)),
                  pl.BlockSpec(gain.shape, lambda b, s: (0, 0)),
                  pl.BlockSpec((tab_rows, table.shape[1]), lambda b, s: (rb(b, s), 0))],
        out_specs=[pl.BlockSpec((ts, gw), lambda b, s: (rb(b, s), 0)),
                   pl.BlockSpec((None, SUBLANES, table.shape[1]), lambda b, s: (rb(b, s), 0, 0))],
        out_shape=[jax.ShapeDtypeStruct((t, gw), BF16),
                   jax.ShapeDtypeStruct((batch * nb, SUBLANES, table.shape[1]), F32)],
        scratch_shapes=[pltpu.VMEM((kw // (2 * dk), dv, 2 * dk), F32)],
        compiler_params=_params("arbitrary", "arbitrary"),
        name="gla",
    )(proj, proj, proj, proj, proj, wup, bup, gain, table)


def _group_rms(x, ones_bd, dh):
    ss = _dot_split(x * x, ones_bd, 2)
    return x * lax.rsqrt(ss * (1.0 / dh) + NORM_EPS)


def _rope(x, cos, sin_signed, dh):
    half = dh // 2
    width = x.shape[1]
    first_half = (lax.broadcasted_iota(jnp.int32, x.shape, 1) & (dh - 1)) < half
    partner = jnp.where(first_half, pltpu.roll(x, width - half, 1), pltpu.roll(x, half, 1))
    return x * cos + partner * sin_signed


def _swa_kernel(q_ref, kc_ref, kp_ref, vc_ref, vp_ref, cc_ref, cp_ref, sc_ref, sp_ref,
                gq_ref, gk_ref, go_ref, sink_ref, ones_ref, sel_ref, tab_ref, o_ref, tab_out_ref, tab_scale_ref,
                *, n_groups, dh):
    W, G = SWA_BLOCK, SWA_GROUP
    n = pl.program_id(1)
    gw = G * dh
    kvw = n_groups * dh
    pairs = G // 2
    tab_out_ref[...], scale = _quantize_rows(tab_ref[...], axis=1)
    tab_scale_ref[...] = jnp.broadcast_to(scale, tab_scale_ref.shape)

    k_all = jnp.concatenate([kp_ref[...], kc_ref[...]], axis=0)
    cos_k = jnp.concatenate([cp_ref[:, :kvw], cc_ref[:, :kvw]], axis=0)
    sin_k = jnp.concatenate([sp_ref[:, :kvw], sc_ref[:, :kvw]], axis=0)
    k_r = _rope(_group_rms(k_all, ones_ref[:kvw, :kvw], dh) * gk_ref[...], cos_k, sin_k, dh)
    k_t = k_r.T.astype(BF16)
    v_all = jnp.concatenate([vp_ref[...], vc_ref[...]], axis=0).astype(BF16)

    qi = lax.broadcasted_iota(jnp.int32, (2 * W, 2 * W), 0) & (W - 1)
    kj = lax.broadcasted_iota(jnp.int32, (2 * W, 2 * W), 1)
    rel = kj - qi - 1
    first_key = jnp.where(n > 0, 0, W)
    valid = (rel >= 0) & (rel < W) & (kj >= first_key)
    head0 = lax.broadcasted_iota(jnp.int32, (W, 2 * dh), 1) < dh
    ones_cols = jnp.ones((2 * W, 2 * dh), BF16)

    for g in range(n_groups):
        q_n = _group_rms(q_ref[:, gw * g:gw * (g + 1)], ones_ref[...], dh) * gq_ref[...]
        q_r = _rope(q_n, cc_ref[...], sc_ref[...], dh)
        k_tg = k_t[dh * g:dh * (g + 1), :]
        k_t2 = jnp.concatenate([k_tg, k_tg], axis=0)
        v3 = jnp.concatenate([_dot(v_all, sel_ref[g]).astype(BF16), ones_cols], axis=1)
        outs = []
        for pp in range(pairs):
            qp = q_r[:, 2 * dh * pp:2 * dh * (pp + 1)]
            q2 = jnp.concatenate([jnp.where(head0, qp, 0.0), jnp.where(head0, 0.0, qp)], axis=0).astype(BF16)
            s = jnp.where(valid, _dot(q2, k_t2) * (dh ** -0.5), -jnp.inf)
            sink = sink_ref[pairs * g + pp]
            m = jnp.maximum(jnp.max(s, axis=-1, keepdims=True), sink)
            r = _dot(jnp.exp(s - m).astype(BF16), v3)
            o2 = r[:, :2 * dh] / (r[:, 2 * dh:] + jnp.exp(sink - m))
            outs.append(jnp.where(head0, o2[:W], o2[W:]))
        o = jnp.concatenate(outs, axis=1)
        o_ref[:, gw * g:gw * (g + 1)] = (_group_rms(o, ones_ref[...], dh) * go_ref[...]).astype(BF16)


def _swa_constants(dh, n_groups, seq):
    G = SWA_GROUP
    gw = G * dh
    half = dh // 2
    i = jnp.arange(gw)
    ones_bd = (i[:, None] // dh == i[None, :] // dh).astype(BF16)
    kvw = n_groups * dh
    j = jnp.arange(2 * dh)
    sel = (jnp.arange(kvw)[None, :, None] == (dh * jnp.arange(n_groups)[:, None, None] + (j % dh)[None, None, :])).astype(BF16)
    inv = ROPE_THETA ** (-jnp.arange(half, dtype=F32) / half)
    ang = jnp.arange(seq, dtype=F32)[:, None] * inv[None, :]
    cos = jnp.tile(jnp.cos(ang), (1, 2 * G))
    sin = jnp.tile(jnp.concatenate([-jnp.sin(ang), jnp.sin(ang)], axis=1), (1, G))
    return ones_bd, sel, cos, sin


def _swa(proj, q_gain, k_gain, o_gain, sinks, table, batch, seq, q_col, qw, kvw, dh):
    t = proj.shape[0]
    tab_rows = table.shape[0] // (batch * (seq // SWA_BLOCK))
    assert tab_rows * batch * (seq // SWA_BLOCK) == table.shape[0] and tab_rows % (4 * SUBLANES) == 0
    tab_spec = pl.BlockSpec((tab_rows, table.shape[1]), lambda b, s: (b * (seq // SWA_BLOCK) + s, 0))
    scale_spec = pl.BlockSpec((tab_rows, LANES), lambda b, s: (b * (seq // SWA_BLOCK) + s, 0))
    W, G = SWA_BLOCK, SWA_GROUP
    gw = G * dh
    n_groups = kvw // dh
    nb = seq // W
    assert dh & (dh - 1) == 0 and 2 * dh == LANES
    ones_bd, sel, cos, sin = _swa_constants(dh, n_groups, seq)
    gq = jnp.tile(q_gain, G)[None, :]
    gk = jnp.tile(k_gain, n_groups)[None, :]
    go = jnp.tile(o_gain, G)[None, :]
    sink_cols = jnp.repeat(sinks.reshape(-1, 2), W, axis=1)[:, :, None]
    cur = lambda b, s: b * nb + s
    prev = lambda b, s: b * nb + jnp.maximum(s - 1, 0)
    k_blk = (q_col + qw) // kvw
    const = lambda a: pl.BlockSpec(a.shape, lambda b, s: (0,) * a.ndim)
    kern = functools.partial(_swa_kernel, n_groups=n_groups, dh=dh)
    return pl.pallas_call(
        kern,
        grid=(batch, nb),
        in_specs=[pl.BlockSpec((W, qw), lambda b, s: (cur(b, s), q_col // qw)),
                  pl.BlockSpec((W, kvw), lambda b, s: (cur(b, s), k_blk)),
                  pl.BlockSpec((W, kvw), lambda b, s: (prev(b, s), k_blk)),
                  pl.BlockSpec((W, kvw), lambda b, s: (cur(b, s), k_blk + 1)),
                  pl.BlockSpec((W, kvw), lambda b, s: (prev(b, s), k_blk + 1)),
                  pl.BlockSpec((W, gw), lambda b, s: (s, 0)),
                  pl.BlockSpec((W, gw), lambda b, s: (jnp.maximum(s - 1, 0), 0)),
                  pl.BlockSpec((W, gw), lambda b, s: (s, 0)),
                  pl.BlockSpec((W, gw), lambda b, s: (jnp.maximum(s - 1, 0), 0)),
                  const(gq), const(gk), const(go), const(sink_cols), const(ones_bd), const(sel), tab_spec],
        out_specs=[pl.BlockSpec((W, qw), lambda b, s: (cur(b, s), 0)), tab_spec, scale_spec],
        out_shape=[jax.ShapeDtypeStruct((t, qw), BF16), jax.ShapeDtypeStruct(table.shape, F8),
                   jax.ShapeDtypeStruct((table.shape[0], LANES), F32)],
        compiler_params=_params("arbitrary", "arbitrary"),
        name="swa",
    )(proj, proj, proj, proj, proj, cos, cos, sin, sin, gq, gk, go, sink_cols, ones_bd, sel, table)


def _outproj_kernel(og_ref, os_ref, wg_ref, ws_ref, x_ref, g_ref, o_ref, wgb_ref, wsb_ref):
    @pl.when(pl.program_id(1) == 0)
    def _():
        wgb_ref[...] = wg_ref[...].astype(BF16)
        wsb_ref[...] = ws_ref[...].astype(BF16)

    mix = _dot(og_ref[...], wgb_ref[...]) + _dot(os_ref[...], wsb_ref[...])
    o_ref[...] = x_ref[...] + g_ref[...] * mix


def _outproj(o_gla, o_swa, w_out, x2, mod, seq, tm, tn):
    t, d = x2.shape
    half = o_gla.shape[1]
    return pl.pallas_call(
        _outproj_kernel,
        grid=(d // tn, t // tm),
        in_specs=[pl.BlockSpec((tm, half), lambda j, i: (i, 0)),
                  pl.BlockSpec((tm, half), lambda j, i: (i, 0)),
                  pl.BlockSpec((half, tn), lambda j, i: (0, j)),
                  pl.BlockSpec((half, tn), lambda j, i: (1, j)),
                  pl.BlockSpec((tm, tn), lambda j, i: (i, j)),
                  pl.BlockSpec((None, None, 1, tn), lambda j, i: (2, (i * tm) // seq, 0, j))],
        out_specs=pl.BlockSpec((tm, tn), lambda j, i: (i, j)),
        out_shape=jax.ShapeDtypeStruct((t, d), F32),
        scratch_shapes=[pltpu.VMEM((half, tn), BF16), pltpu.VMEM((half, tn), BF16)],
        compiler_params=_params("arbitrary", "arbitrary"),
        name="outproj",
    )(o_gla, o_swa, w_out, w_out, x2, mod)


def _sort_network(n):
    pairs = []

    def merge(lo, hi, r):
        step = r * 2
        if step < hi - lo:
            merge(lo, hi, step)
            merge(lo + r, hi, step)
            pairs.extend((i, i + r) for i in range(lo + r, hi - r, step))
        else:
            pairs.append((lo, lo + r))

    def sort(lo, hi):
        if hi - lo >= 1:
            mid = lo + (hi - lo) // 2
            sort(lo, mid)
            sort(mid + 1, hi)
            merge(lo, hi, 1)

    sort(0, n - 1)
    return pairs


def _top_values(s, k):
    blocks = [s[SUBLANES * r:SUBLANES * (r + 1)] for r in range(s.shape[0] // SUBLANES)]
    assert len(blocks) <= k and k & (k - 1) == 0 and s.shape[0] % SUBLANES == 0
    blocks += [jnp.full_like(blocks[0], -jnp.inf)] * (k - len(blocks))

    def order(i, j):
        blocks[i], blocks[j] = jnp.maximum(blocks[i], blocks[j]), jnp.minimum(blocks[i], blocks[j])

    for i, j in _sort_network(k):
        order(i, j)
    shift = SUBLANES // 2
    while shift >= 1:
        other = [pltpu.roll(b, shift, 0) for b in blocks]
        blocks[:] = [jnp.maximum(blocks[r], other[k - 1 - r]) for r in range(k)]
        dist = k // 2
        while dist >= 1:
            for r in range(k):
                if r & dist == 0:
                    order(r, r + dist)
            dist //= 2
        shift //= 2
    return [b[0:1] for b in blocks]


def _peerq_kernel(x_ref, sh_ref, sc_ref, w_ref, k1_ref, k2_ref, tab_ref, tinv_ref, yt_ref, ys_ref, s1_ref, s2_ref,
                  tc_ref, tab_out_ref, y_ref, *, heads_per_step, half):
    tab_out_ref[...] = (tab_ref[...] * tinv_ref[...]).astype(F8)

    @pl.when(pl.program_id(1) == 0)
    def _():
        y = _modulated_rms(x_ref[...], sc_ref[...], sh_ref[...])
        y_ref[...] = y.astype(BF16)
        yt_ref[...], scale = _quantize_rows(y.T, axis=0)
        ys_ref[...] = jnp.broadcast_to(scale, ys_ref.shape)

    K = PEER_TOPK
    q = _dot(y_ref[...], w_ref[...])
    tm = q.shape[0]
    for hh in range(heads_per_step):
        q1 = q[:, 2 * half * hh:2 * half * hh + half].astype(BF16)
        q2 = q[:, 2 * half * hh + half:2 * half * (hh + 1)].astype(BF16)
        s1 = _dot_nt(k1_ref[...], q1)
        s2 = _dot_nt(k2_ref[...], q2)
        s1_ref[hh] = s1
        s2_ref[hh] = s2
        v1 = _top_values(s1, K)
        v2 = jnp.concatenate(_top_values(s2, K), axis=0)
        cand = [v1[0] + v2]
        for i in range(1, K // 2):
            cand.append(v1[i] + v2[:K // 2])
        cand.append(jnp.concatenate(v1[K // 2:], axis=0) + v2[0:1])
        top = _top_values(jnp.concatenate(cand, axis=0), K)
        m = top[0]
        z = jnp.exp(top[0] - m)
        for r in range(1, K):
            z = z + jnp.exp(top[r] - m)
        tc_ref[hh] = jnp.concatenate([top[K - 1], m + jnp.log(z), jnp.zeros((SUBLANES - 2, tm), F32)], axis=0)


def _peerq(x1, mod, w_q, k1, k2, table, table_inv_scale, seq, tm, n_heads):
    t, d = x1.shape
    nq = w_q.shape[1]
    n_keys, half = k1.shape
    hps = 2
    tn = hps * 2 * half
    n_j = nq // tn
    tab_rows = table.shape[0] // ((t // tm) * n_j)
    assert tab_rows * (t // tm) * n_j == table.shape[0] and tab_rows % (4 * SUBLANES) == 0
    tab_spec = pl.BlockSpec((tab_rows, table.shape[1]), lambda i, j: (i * n_j + j, 0))
    mod_spec = lambda k: pl.BlockSpec((None, None, 1, d), lambda i, j: (k, (i * tm) // seq, 0, 0))
    kern = functools.partial(_peerq_kernel, heads_per_step=hps, half=half)
    return pl.pallas_call(
        kern,
        grid=(t // tm, n_j),
        in_specs=[pl.BlockSpec((tm, d), lambda i, j: (i, 0)),
                  mod_spec(3), mod_spec(4),
                  pl.BlockSpec((d, tn), lambda i, j: (0, j)),
                  pl.BlockSpec(k1.shape, lambda i, j: (0, 0)),
                  pl.BlockSpec(k2.shape, lambda i, j: (0, 0)),
                  tab_spec,
                  pl.BlockSpec(table_inv_scale.shape, lambda i, j: (0, 0))],
        out_specs=[pl.BlockSpec((d, tm), lambda i, j: (0, i)),
                   pl.BlockSpec((SUBLANES, tm), lambda i, j: (0, i)),
                   pl.BlockSpec((hps, n_keys, tm), lambda i, j: (j, 0, i)),
                   pl.BlockSpec((hps, n_keys, tm), lambda i, j: (j, 0, i)),
                   pl.BlockSpec((hps, SUBLANES, tm), lambda i, j: (j, 0, i)),
                   tab_spec],
        out_shape=[jax.ShapeDtypeStruct((d, t), F8),
                   jax.ShapeDtypeStruct((SUBLANES, t), F32),
                   jax.ShapeDtypeStruct((n_heads, n_keys, t), F32),
                   jax.ShapeDtypeStruct((n_heads, n_keys, t), F32),
                   jax.ShapeDtypeStruct((n_heads, SUBLANES, t), F32),
                   jax.ShapeDtypeStruct(table.shape, F8)],
        scratch_shapes=[pltpu.VMEM((tm, d), BF16)],
        compiler_params=_params("arbitrary", "arbitrary"),
        name="peerq",
    )(x1, mod, mod, w_q, k1, k2, table, table_inv_scale)


def _peer_kernel(yt_ref, ys_ref, dn_ref, ds_ref, up_ref, s1_ref, s2_ref, tc_ref, x_ref, g_ref, o_ref,
                 hd_ref, gate_ref, hst_ref, st_ref, *, n_heads, n_keys, te):
    s = pl.program_id(1)
    n = pl.num_programs(1) - 1
    d, tm = yt_ref.shape

    @pl.when((pl.program_id(0) == 0) & (s == 0))
    def _():
        hst_ref[...] = jnp.zeros_like(hst_ref)
        st_ref[...] = jnp.zeros_like(st_ref)

    @pl.when(s == 0)
    def _():
        o_ref[...] = jnp.zeros_like(o_ref)

    per_step = te // n_keys

    @pl.when(s < n)
    def _():
        n_kc = d // MXU_TILE
        per_a = n_kc // per_step
        tw = tm // per_a
        acc = None
        for kc in range(n_kc):
            ks = slice(MXU_TILE * kc, MXU_TILE * (kc + 1))
            part = _dot(dn_ref[:, ks], yt_ref[ks, :])
            acc = part if acc is None else acc + part
            ai, q = divmod(kc, per_a)
            a = s * per_step + ai
            cols = slice(tw * q, tw * (q + 1))
            gate = None
            for h in range(n_heads):
                cand = s1_ref[h, pl.ds(a, 1), :][:, cols] + s2_ref[h, :, cols]
                term = jnp.where(cand >= tc_ref[h, 0:1, cols], jnp.exp(cand - tc_ref[h, 1:2, cols]), 0.0)
                gate = term if gate is None else gate + term
            gate_ref[n_keys * ai:n_keys * (ai + 1), cols] = gate
        hd_ref[...] = acc

    cur = s % 2
    th = tm // 2
    tok_scale = _dot_tn(st_ref[1 - cur].astype(BF16), jnp.ones((SUBLANES, LANES), BF16))
    tok_scale = jnp.tile(tok_scale, (1, PEER_OUT_CHUNK // LANES))
    for j in range(d // PEER_OUT_CHUNK):
        cols = slice(PEER_OUT_CHUNK * j, PEER_OUT_CHUNK * (j + 1))
        o_ref[:, cols] += _dot(hst_ref[1 - cur], up_ref[:, cols]) * tok_scale

    pieces = [(slice(n_keys * ai, n_keys * (ai + 1)), slice(th * half, th * (half + 1)), half)
              for ai in range(per_step) for half in range(2)]
    amax = [None, None]
    for rows, tok, half in pieces:
        x = hd_ref[rows, tok] * jnp.tile(ds_ref[rows, :], (1, th // LANES)) * ys_ref[0:1, tok]
        hs = 0.5 * x * (1.0 + lax.erf(x * (2.0 ** -0.5))) * gate_ref[rows, tok]
        hd_ref[rows, tok] = hs
        m = jnp.max(jnp.abs(hs), axis=0, keepdims=True)
        amax[half] = m if amax[half] is None else jnp.maximum(amax[half], m)
    scale = _pow2_scale(jnp.concatenate(amax, axis=1))
    inv = 1.0 / scale
    for rows, tok, half in pieces:
        hst_ref[cur, tok, rows] = (hd_ref[rows, tok] * inv[:, tok]).T.astype(F8)
    st_ref[cur] = jnp.concatenate([scale, jnp.zeros((SUBLANES - 1, tm), F32)], axis=0)

    @pl.when(s == n)
    def _():
        o_ref[...] = x_ref[...] + g_ref[...] * o_ref[...]
        hst_ref[1] = jnp.zeros_like(hst_ref[1])


def _peer(y_t, y_scale, down, down_scale, up, s1, s2, tc, x1, gate_scale, seq, tm, te):
    d, t = y_t.shape
    n = down.shape[0] // te
    n_heads, n_keys, _ = s1.shape
    assert (d // MXU_TILE) % (te // n_keys) == 0 and d % PEER_OUT_CHUNK == 0
    once = dict(pipeline_mode=pl.Buffered(1))
    kern = functools.partial(_peer_kernel, n_heads=n_heads, n_keys=n_keys, te=te)
    return pl.pallas_call(
        kern,
        grid=(t // tm, n + 1),
        in_specs=[pl.BlockSpec((d, tm), lambda i, s: (0, i), **once),
                  pl.BlockSpec((SUBLANES, tm), lambda i, s: (0, i), **once),
                  pl.BlockSpec((te, d), lambda i, s: (jnp.minimum(s, n - 1), 0)),
                  pl.BlockSpec((te, LANES), lambda i, s: (jnp.minimum(s, n - 1), 0)),
                  pl.BlockSpec((te, d), lambda i, s: (jnp.maximum(s - 1, 0), 0)),
                  pl.BlockSpec((n_heads, n_keys, tm), lambda i, s: (0, 0, i), **once),
                  pl.BlockSpec((n_heads, n_keys, tm), lambda i, s: (0, 0, i), **once),
                  pl.BlockSpec((n_heads, SUBLANES, tm), lambda i, s: (0, 0, i), **once),
                  pl.BlockSpec((tm, d), lambda i, s: (i, 0), **once),
                  pl.BlockSpec((None, None, 1, d), lambda i, s: (0, (i * tm) // seq, 0, 0))],
        out_specs=pl.BlockSpec((tm, d), lambda i, s: (i, 0)),
        out_shape=jax.ShapeDtypeStruct((t, d), F32),
        scratch_shapes=[pltpu.VMEM((te, tm), F32), pltpu.VMEM((te, tm), F32), pltpu.VMEM((2, tm, te), F8),
                        pltpu.VMEM((2, SUBLANES, tm), F32)],
        compiler_params=_params("arbitrary", "arbitrary"),
        name="peer",
    )(y_t, y_scale, down, down_scale, up, s1, s2, tc, x1, gate_scale)


def _tile(total, want):
    return want if total % want == 0 else total


def kernel(x, c, w_ada, b_ada, w_in, w_gla_gate_up, b_gla_gate, gla_out_norm, swa_q_norm, swa_k_norm, swa_sinks, swa_out_norm, w_out, w_peer_q, peer_sub_keys_1, peer_sub_keys_2, peer_expert_down, peer_expert_up):
    batch, seq, d = x.shape
    t = batch * seq
    depth = w_ada.shape[0]
    kw = w_gla_gate_up.shape[2]
    dv = gla_out_norm.shape[1]
    dk = dv // 2
    gw = (kw // dk) * dv
    dh = swa_q_norm.shape[1]
    qw = swa_sinks.shape[1] * dh
    kvw = qw // SWA_GROUP
    in_width = w_in.shape[2]
    ga_old = 2 * kw + 2 * gw
    assert in_width == ga_old + GLA_GATE_RANK + qw + 2 * kvw
    assert gw == 2 * kw and qw == gw and d == gw + qw and 2 * dk == LANES and dv == LANES
    assert seq % SWA_BLOCK == 0 and kvw % LANES == 0
    n_keys, half = peer_sub_keys_1.shape[1:]
    n_heads = w_peer_q.shape[2] // (2 * half)
    assert n_keys == LANES and half == LANES and n_heads % 2 == 0

    q_col = ga_old
    n_tail = qw + 2 * kvw
    ga_col = ga_old + n_tail
    tn_in = 512 if ga_old % 512 == 0 and n_tail % 512 == 0 else 256
    assert ga_old % tn_in == 0 and n_tail % tn_in == 0 and qw >= tn_in

    tm_in = _tile(t, 1024)
    ts_gla = _tile(seq, 256)
    tm_out = _tile(seq, 1024)
    tm_peer = _tile(seq, 512)
    te = 512

    c_pad = jnp.pad(c, ((0, -batch % SUBLANES), (0, 0)))
    x2 = x.reshape(t, d)
    for l in range(depth):
        mod = _ada(c_pad, w_ada[l], b_ada[l][None, :])
        mod = mod[:batch].reshape(batch, N_MOD, d).transpose(1, 0, 2)[:, :, None, :]

        h = _modulate(x2, mod, seq, tm_peer)
        proj = _inproj(h, w_in[l].T, tm_in, tn_in, ga_old, n_tail)

        wup = jnp.pad(w_gla_gate_up[l], ((0, LANES - GLA_GATE_RANK), (0, 0))).astype(BF16)
        o_gla, up_cmax = _gla(proj, wup, b_gla_gate[l][None, :], gla_out_norm[l][None, :], peer_expert_up[l],
                              batch, seq, ts_gla, kw, gw, ga_col, dk, dv)
        up_scale = jnp.maximum(jnp.max(up_cmax, axis=(0, 1)), F8_AMAX_FLOOR)[None, :] * (1.0 / F8_MAX)
        o_swa, down, down_scale = _swa(proj, swa_q_norm[l], swa_k_norm[l], swa_out_norm[l], swa_sinks[l],
                           peer_expert_down[l], batch, seq, q_col, qw, kvw, dh)
        x1 = _outproj(o_gla, o_swa, w_out[l], x2, mod, seq, tm_out, 512)

        y_t, y_scale, s1, s2, tc, up = _peerq(x1, mod, w_peer_q[l].astype(BF16), peer_sub_keys_1[l].astype(BF16),
                                     peer_sub_keys_2[l].astype(BF16), peer_expert_up[l], 1.0 / up_scale,
                                     seq, tm_peer, n_heads)
        x2 = _peer(y_t, y_scale, down, down_scale, up, s1, s2, tc, x1, (mod[5] * up_scale)[None],
                   seq, tm_peer, te)
    return x2.reshape(batch, seq, d)
```

```python
import functools

import jax
import jax.numpy as jnp
from jax import lax
from jax.experimental import pallas as pl
from jax.experimental.pallas import tpu as pltpu

F32 = jnp.float32
BF16 = jnp.bfloat16
F8 = jnp.float8_e4m3fn
F8_MAX = 448.0
F8_AMAX_FLOOR = 2.0 ** -100

NORM_EPS = 1e-6
N_MOD = 6
GLA_GATE_RANK = 16
GLA_GATE_TEMP = 16.0
GLA_CHUNK = 64
SWA_BLOCK = 128
SWA_GROUP = 8
ROPE_THETA = 10000.0
PEER_TOPK = 16

LANES = 128
SUBLANES = 8
MXU_TILE = 256
PEER_OUT_CHUNK = 512
VMEM_LIMIT_BYTES = 60 * 1024 * 1024


def _params(*semantics):
    return pltpu.CompilerParams(dimension_semantics=semantics, vmem_limit_bytes=VMEM_LIMIT_BYTES)


def _dot(a, b):
    return jnp.dot(a, b, preferred_element_type=F32)


def _dot_nt(a, b):
    return lax.dot_general(a, b, (((1,), (1,)), ((), ())), preferred_element_type=F32)


def _dot_tn(a, b):
    return lax.dot_general(a, b, (((0,), (0,)), ((), ())), preferred_element_type=F32)


def _split(x, parts):
    out = []
    for _ in range(parts - 1):
        hi = x.astype(BF16)
        out.append(hi)
        x = x - hi.astype(F32)
    out.append(x.astype(BF16))
    return out


def _dot_split(x, m, parts):
    acc = None
    for p in _split(x, parts):
        t = _dot(p, m)
        acc = t if acc is None else acc + t
    return acc


def _quantize_rows(x, axis):
    amax = jnp.max(jnp.abs(x), axis=axis, keepdims=True)
    scale = jnp.maximum(amax, F8_AMAX_FLOOR) * (1.0 / F8_MAX)
    return (x * (1.0 / scale)).astype(F8), scale


def _pow2_scale(amax):
    bits = lax.bitcast_convert_type(jnp.maximum(amax, F8_AMAX_FLOOR), jnp.int32)
    above = lax.bitcast_convert_type((bits & 0x7F800000) + 0x00800000, F32)
    return above * (1.0 / 256.0)


def _modulated_rms(x, scale, shift):
    ms = jnp.mean(x * x, axis=-1, keepdims=True)
    return x * lax.rsqrt(ms + NORM_EPS) * (1.0 + scale) + shift


def _ada_kernel(c_ref, w_ref, b_ref, o_ref):
    c = c_ref[...]
    act = (c * jax.nn.sigmoid(c)).astype(BF16)
    o_ref[...] = _dot(act, w_ref[...].astype(BF16)) + b_ref[...]


def _ada(c_pad, w, b):
    rows, d = c_pad.shape
    n = w.shape[1]
    tn = 512
    return pl.pallas_call(
        _ada_kernel,
        grid=(n // tn,),
        in_specs=[pl.BlockSpec((rows, d), lambda j: (0, 0)),
                  pl.BlockSpec((d, tn), lambda j: (0, j)),
                  pl.BlockSpec((1, tn), lambda j: (0, j))],
        out_specs=pl.BlockSpec((rows, tn), lambda j: (0, j)),
        out_shape=jax.ShapeDtypeStruct((rows, n), F32),
        compiler_params=_params("arbitrary"),
        name="ada",
    )(c_pad, w, b)


def _modulate_kernel(x_ref, sh_ref, sc_ref, o_ref):
    o_ref[...] = _modulated_rms(x_ref[...], sc_ref[...], sh_ref[...]).astype(BF16)


def _modulate(x2, mod, seq, tm):
    t, d = x2.shape
    mod_spec = lambda k: pl.BlockSpec((None, None, 1, d), lambda i: (k, (i * tm) // seq, 0, 0))
    return pl.pallas_call(
        _modulate_kernel,
        grid=(t // tm,),
        in_specs=[pl.BlockSpec((tm, d), lambda i: (i, 0)), mod_spec(0), mod_spec(1)],
        out_specs=pl.BlockSpec((tm, d), lambda i: (i, 0)),
        out_shape=jax.ShapeDtypeStruct((t, d), BF16),
        compiler_params=_params("arbitrary"),
        name="modulate",
    )(x2, mod, mod)


def _inproj_kernel(h_ref, wt_ref, o_ref, w_ref):
    @pl.when(pl.program_id(1) == 0)
    def _():
        w_ref[...] = wt_ref[...].T.astype(BF16)

    o_ref[...] = _dot(h_ref[...], w_ref[...])


def _inproj(h, w_t, tm, tn, ga_old, n_tail):
    t, d = h.shape
    n_main = ga_old // tn
    n_sw = n_tail // tn

    def src_row(j):
        s = SUBLANES
        return s * jnp.where(j < n_main, j * (tn // s),
                             jnp.where(j < n_main + n_sw, (ga_old + GLA_GATE_RANK) // s + (j - n_main) * (tn // s),
                                       ga_old // s))

    n_tiles = n_main + n_sw + 1
    return pl.pallas_call(
        _inproj_kernel,
        grid=(n_tiles, t // tm),
        in_specs=[pl.BlockSpec((tm, d), lambda j, i: (i, 0)),
                  pl.BlockSpec((pl.Element(tn), pl.Element(d)), lambda j, i: (src_row(j), 0))],
        out_specs=pl.BlockSpec((tm, tn), lambda j, i: (i, j)),
        out_shape=jax.ShapeDtypeStruct((t, n_tiles * tn), F32),
        scratch_shapes=[pltpu.VMEM((d, tn), BF16)],
        compiler_params=_params("arbitrary", "arbitrary"),
        name="inproj",
    )(h, w_t)


def _gla_kernel(q_ref, k_ref, v_ref, g_ref, a_ref, wup_ref, bup_ref, gain_ref, tab_ref, o_ref, cmax_ref, st_ref,
                *, n_pairs, n_chunks, dk, dv):
    cmax_ref[...] = jnp.broadcast_to(jnp.max(jnp.abs(tab_ref[...]), axis=0, keepdims=True), cmax_ref.shape)

    @pl.when(pl.program_id(1) == 0)
    def _():
        st_ref[...] = jnp.zeros_like(st_ref)

    C = GLA_CHUNK
    row = lax.broadcasted_iota(jnp.int32, (C, C), 0)
    col = lax.broadcasted_iota(jnp.int32, (C, C), 1)
    tril = row >= col
    cum_op = tril.astype(BF16)
    causal2 = jnp.concatenate([tril, tril], axis=0)
    head0 = lax.broadcasted_iota(jnp.int32, (C, 2 * dk), 1) < dk
    head0_st = lax.broadcasted_iota(jnp.int32, (dv, 2 * dk), 1) < dk
    scale = dk ** -0.5
    gain = gain_ref[...]

    def stack_heads(x):
        return jnp.concatenate([jnp.where(head0, x, 0.0), jnp.where(head0, 0.0, x)], axis=0).astype(BF16)

    def chunk(c, carry):
        rows = pl.ds(pl.multiple_of(c * C, C), C)
        z = _dot(a_ref[rows, :].astype(BF16), wup_ref[...]) + bup_ref[...]
        log_a = (jnp.minimum(z, 0.0) - jnp.log1p(jnp.exp(-jnp.abs(z)))) * (1.0 / GLA_GATE_TEMP)
        b = _dot_split_left(cum_op, log_a)
        b_last = b[C - 1:C, :]
        mid = 0.5 * b_last
        qt = q_ref[rows, :] * scale * jnp.exp(b - mid)
        kt = k_ref[rows, :] * jnp.exp(mid - b)
        qi = qt * jnp.exp(mid)
        kl = (kt * jnp.exp(b_last - mid)).astype(BF16)
        kt = kt.astype(BF16)
        decay = jnp.exp(b_last)
        pair = lambda x, p: x[:, 2 * dk * p:2 * dk * (p + 1)]
        heads = range(2 * n_pairs)
        st = [st_ref[p] for p in range(n_pairs)]
        attn = [_dot_nt(stack_heads(pair(qt, p)), pair(kt, p)) for p in range(n_pairs)]
        inter = [_dot_nt(stack_heads(pair(qi, p)), st[p].astype(BF16)) for p in range(n_pairs)]
        attn = [jnp.where(causal2, a, 0.0).astype(BF16) for a in attn]
        v = [v_ref[rows, dv * h:dv * (h + 1)].astype(BF16) for h in heads]
        half = lambda x, h: x[h // 2][C * (h % 2):C * (h % 2 + 1)]
        out = [_dot(half(attn, h), v[h]) + half(inter, h) for h in heads]
        upd = [_dot_tn(v[h], pair(kl, h // 2)) for h in heads]
        for h in heads:
            o = out[h]
            on = o * lax.rsqrt(jnp.mean(o * o, axis=-1, keepdims=True) + NORM_EPS) * gain
            gg = g_ref[rows, dv * h:dv * (h + 1)]
            o_ref[rows, dv * h:dv * (h + 1)] = (on * (gg * jax.nn.sigmoid(gg))).astype(BF16)
        for p in range(n_pairs):
            st_ref[p] = st[p] * pair(decay, p) + jnp.where(head0_st, upd[2 * p], upd[2 * p + 1])
        return carry

    lax.fori_loop(0, n_chunks, chunk, 0)


def _dot_split_left(m, x):
    acc = None
    for p in _split(x, 3):
        t = _dot(m, p)
        acc = t if acc is None else acc + t
    return acc


def _gla(proj, wup, bup, gain, table, batch, seq, ts, kw, gw, ga_col, dk, dv):
    t = proj.shape[0]
    nb = seq // ts
    tab_rows = table.shape[0] // (batch * nb)
    assert tab_rows * batch * nb == table.shape[0] and tab_rows % SUBLANES == 0
    rb = lambda b, s: b * nb + s
    kern = functools.partial(_gla_kernel, n_pairs=kw // (2 * dk), n_chunks=ts // GLA_CHUNK, dk=dk, dv=dv)
    return pl.pallas_call(
        kern,
        grid=(batch, nb),
        in_specs=[pl.BlockSpec((ts, kw), lambda b, s: (rb(b, s), 0)),
                  pl.BlockSpec((ts, kw), lambda b, s: (rb(b, s), 1)),
                  pl.BlockSpec((ts, gw), lambda b, s: (rb(b, s), (2 * kw) // gw)),
                  pl.BlockSpec((ts, gw), lambda b, s: (rb(b, s), (2 * kw) // gw + 1)),
                  pl.BlockSpec((ts, LANES), lambda b, s: (rb(b, s), ga_col // LANES)),
                  pl.BlockSpec(wup.shape, lambda b, s: (0, 0)),
                  pl.BlockSpec(bup.shape, lambda b, s: (0, 0)),
                  pl.BlockSpec(gain.shape, lambda b, s: (0, 0)),
                  pl.BlockSpec((tab_rows, table.shape[1]), lambda b, s: (rb(b, s), 0))],
        out_specs=[pl.BlockSpec((ts, gw), lambda b, s: (rb(b, s), 0)),
                   pl.BlockSpec((None, SUBLANES, table.shape[1]), lambda b, s: (rb(b, s), 0, 0))],
        out_shape=[jax.ShapeDtypeStruct((t, gw), BF16),
                   jax.ShapeDtypeStruct((batch * nb, SUBLANES, table.shape[1]), F32)],
        scratch_shapes=[pltpu.VMEM((kw // (2 * dk), dv, 2 * dk), F32)],
        compiler_params=_params("arbitrary", "arbitrary"),
        name="gla",
    )(proj, proj, proj, proj, proj, wup, bup, gain, table)


def _group_rms(x, ones_bd, dh):
    ss = _dot_split(x * x, ones_bd, 2)
    return x * lax.rsqrt(ss * (1.0 / dh) + NORM_EPS)


def _rope(x, cos, sin_signed, dh):
    half = dh // 2
    width = x.shape[1]
    first_half = (lax.broadcasted_iota(jnp.int32, x.shape, 1) & (dh - 1)) < half
    partner = jnp.where(first_half, pltpu.roll(x, width - half, 1), pltpu.roll(x, half, 1))
    return x * cos + partner * sin_signed


def _swa_kernel(q_ref, kc_ref, kp_ref, vc_ref, vp_ref, cc_ref, cp_ref, sc_ref, sp_ref,
                gq_ref, gk_ref, go_ref, sink_ref, ones_ref, sel_ref, tab_ref, o_ref, tab_out_ref, tab_scale_ref,
                *, n_groups, dh):
    W, G = SWA_BLOCK, SWA_GROUP
    n = pl.program_id(1)
    gw = G * dh
    kvw = n_groups * dh
    pairs = G // 2
    tab_out_ref[...], scale = _quantize_rows(tab_ref[...], axis=1)
    tab_scale_ref[...] = jnp.broadcast_to(scale, tab_scale_ref.shape)

    k_all = jnp.concatenate([kp_ref[...], kc_ref[...]], axis=0)
    cos_k = jnp.concatenate([cp_ref[:, :kvw], cc_ref[:, :kvw]], axis=0)
    sin_k = jnp.concatenate([sp_ref[:, :kvw], sc_ref[:, :kvw]], axis=0)
    k_r = _rope(_group_rms(k_all, ones_ref[:kvw, :kvw], dh) * gk_ref[...], cos_k, sin_k, dh)
    k_t = k_r.T.astype(BF16)
    v_all = jnp.concatenate([vp_ref[...], vc_ref[...]], axis=0).astype(BF16)

    qi = lax.broadcasted_iota(jnp.int32, (2 * W, 2 * W), 0) & (W - 1)
    kj = lax.broadcasted_iota(jnp.int32, (2 * W, 2 * W), 1)
    rel = kj - qi - 1
    first_key = jnp.where(n > 0, 0, W)
    valid = (rel >= 0) & (rel < W) & (kj >= first_key)
    head0 = lax.broadcasted_iota(jnp.int32, (W, 2 * dh), 1) < dh
    ones_cols = jnp.ones((2 * W, 2 * dh), BF16)

    for g in range(n_groups):
        q_n = _group_rms(q_ref[:, gw * g:gw * (g + 1)], ones_ref[...], dh) * gq_ref[...]
        q_r = _rope(q_n, cc_ref[...], sc_ref[...], dh)
        k_tg = k_t[dh * g:dh * (g + 1), :]
        k_t2 = jnp.concatenate([k_tg, k_tg], axis=0)
        v3 = jnp.concatenate([_dot(v_all, sel_ref[g]).astype(BF16), ones_cols], axis=1)
        outs = []
        for pp in range(pairs):
            qp = q_r[:, 2 * dh * pp:2 * dh * (pp + 1)]
            q2 = jnp.concatenate([jnp.where(head0, qp, 0.0), jnp.where(head0, 0.0, qp)], axis=0).astype(BF16)
            s = jnp.where(valid, _dot(q2, k_t2) * (dh ** -0.5), -jnp.inf)
            sink = sink_ref[pairs * g + pp]
            m = jnp.maximum(jnp.max(s, axis=-1, keepdims=True), sink)
            r = _dot(jnp.exp(s - m).astype(BF16), v3)
            o2 = r[:, :2 * dh] / (r[:, 2 * dh:] + jnp.exp(sink - m))
            outs.append(jnp.where(head0, o2[:W], o2[W:]))
        o = jnp.concatenate(outs, axis=1)
        o_ref[:, gw * g:gw * (g + 1)] = (_group_rms(o, ones_ref[...], dh) * go_ref[...]).astype(BF16)


def _swa_constants(dh, n_groups, seq):
    G = SWA_GROUP
    gw = G * dh
    half = dh // 2
    i = jnp.arange(gw)
    ones_bd = (i[:, None] // dh == i[None, :] // dh).astype(BF16)
    kvw = n_groups * dh
    j = jnp.arange(2 * dh)
    sel = (jnp.arange(kvw)[None, :, None] == (dh * jnp.arange(n_groups)[:, None, None] + (j % dh)[None, None, :])).astype(BF16)
    inv = ROPE_THETA ** (-jnp.arange(half, dtype=F32) / half)
    ang = jnp.arange(seq, dtype=F32)[:, None] * inv[None, :]
    cos = jnp.tile(jnp.cos(ang), (1, 2 * G))
    sin = jnp.tile(jnp.concatenate([-jnp.sin(ang), jnp.sin(ang)], axis=1), (1, G))
    return ones_bd, sel, cos, sin


def _swa(proj, q_gain, k_gain, o_gain, sinks, table, batch, seq, q_col, qw, kvw, dh):
    t = proj.shape[0]
    tab_rows = table.shape[0] // (batch * (seq // SWA_BLOCK))
    assert tab_rows * batch * (seq // SWA_BLOCK) == table.shape[0] and tab_rows % (4 * SUBLANES) == 0
    tab_spec = pl.BlockSpec((tab_rows, table.shape[1]), lambda b, s: (b * (seq // SWA_BLOCK) + s, 0))
    scale_spec = pl.BlockSpec((tab_rows, LANES), lambda b, s: (b * (seq // SWA_BLOCK) + s, 0))
    W, G = SWA_BLOCK, SWA_GROUP
    gw = G * dh
    n_groups = kvw // dh
    nb = seq // W
    assert dh & (dh - 1) == 0 and 2 * dh == LANES
    ones_bd, sel, cos, sin = _swa_constants(dh, n_groups, seq)
    gq = jnp.tile(q_gain, G)[None, :]
    gk = jnp.tile(k_gain, n_groups)[None, :]
    go = jnp.tile(o_gain, G)[None, :]
    sink_cols = jnp.repeat(sinks.reshape(-1, 2), W, axis=1)[:, :, None]
    cur = lambda b, s: b * nb + s
    prev = lambda b, s: b * nb + jnp.maximum(s - 1, 0)
    k_blk = (q_col + qw) // kvw
    const = lambda a: pl.BlockSpec(a.shape, lambda b, s: (0,) * a.ndim)
    kern = functools.partial(_swa_kernel, n_groups=n_groups, dh=dh)
    return pl.pallas_call(
        kern,
        grid=(batch, nb),
        in_specs=[pl.BlockSpec((W, qw), lambda b, s: (cur(b, s), q_col // qw)),
                  pl.BlockSpec((W, kvw), lambda b, s: (cur(b, s), k_blk)),
                  pl.BlockSpec((W, kvw), lambda b, s: (prev(b, s), k_blk)),
                  pl.BlockSpec((W, kvw), lambda b, s: (cur(b, s), k_blk + 1)),
                  pl.BlockSpec((W, kvw), lambda b, s: (prev(b, s), k_blk + 1)),
                  pl.BlockSpec((W, gw), lambda b, s: (s, 0)),
                  pl.BlockSpec((W, gw), lambda b, s: (jnp.maximum(s - 1, 0), 0)),
                  pl.BlockSpec((W, gw), lambda b, s: (s, 0)),
                  pl.BlockSpec((W, gw), lambda b, s: (jnp.maximum(s - 1, 0), 0)),
                  const(gq), const(gk), const(go), const(sink_cols), const(ones_bd), const(sel), tab_spec],
        out_specs=[pl.BlockSpec((W, qw), lambda b, s: (cur(b, s), 0)), tab_spec, scale_spec],
        out_shape=[jax.ShapeDtypeStruct((t, qw), BF16), jax.ShapeDtypeStruct(table.shape, F8),
                   jax.ShapeDtypeStruct((table.shape[0], LANES), F32)],
        compiler_params=_params("arbitrary", "arbitrary"),
        name="swa",
    )(proj, proj, proj, proj, proj, cos, cos, sin, sin, gq, gk, go, sink_cols, ones_bd, sel, table)


def _outproj_kernel(og_ref, os_ref, wg_ref, ws_ref, x_ref, g_ref, o_ref, wgb_ref, wsb_ref):
    @pl.when(pl.program_id(1) == 0)
    def _():
        wgb_ref[...] = wg_ref[...].astype(BF16)
        wsb_ref[...] = ws_ref[...].astype(BF16)

    mix = _dot(og_ref[...], wgb_ref[...]) + _dot(os_ref[...], wsb_ref[...])
    o_ref[...] = x_ref[...] + g_ref[...] * mix


def _outproj(o_gla, o_swa, w_out, x2, mod, seq, tm, tn):
    t, d = x2.shape
    half = o_gla.shape[1]
    return pl.pallas_call(
        _outproj_kernel,
        grid=(d // tn, t // tm),
        in_specs=[pl.BlockSpec((tm, half), lambda j, i: (i, 0)),
                  pl.BlockSpec((tm, half), lambda j, i: (i, 0)),
                  pl.BlockSpec((half, tn), lambda j, i: (0, j)),
                  pl.BlockSpec((half, tn), lambda j, i: (1, j)),
                  pl.BlockSpec((tm, tn), lambda j, i: (i, j)),
                  pl.BlockSpec((None, None, 1, tn), lambda j, i: (2, (i * tm) // seq, 0, j))],
        out_specs=pl.BlockSpec((tm, tn), lambda j, i: (i, j)),
        out_shape=jax.ShapeDtypeStruct((t, d), F32),
        scratch_shapes=[pltpu.VMEM((half, tn), BF16), pltpu.VMEM((half, tn), BF16)],
        compiler_params=_params("arbitrary", "arbitrary"),
        name="outproj",
    )(o_gla, o_swa, w_out, w_out, x2, mod)


def _sort_network(n):
    pairs = []

    def merge(lo, hi, r):
        step = r * 2
        if step < hi - lo:
            merge(lo, hi, step)
            merge(lo + r, hi, step)
            pairs.extend((i, i + r) for i in range(lo + r, hi - r, step))
        else:
            pairs.append((lo, lo + r))

    def sort(lo, hi):
        if hi - lo >= 1:
            mid = lo + (hi - lo) // 2
            sort(lo, mid)
            sort(mid + 1, hi)
            merge(lo, hi, 1)

    sort(0, n - 1)
    return pairs


def _top_values(s, k):
    blocks = [s[SUBLANES * r:SUBLANES * (r + 1)] for r in range(s.shape[0] // SUBLANES)]
    assert len(blocks) <= k and k & (k - 1) == 0 and s.shape[0] % SUBLANES == 0
    blocks += [jnp.full_like(blocks[0], -jnp.inf)] * (k - len(blocks))

    def order(i, j):
        blocks[i], blocks[j] = jnp.maximum(blocks[i], blocks[j]), jnp.minimum(blocks[i], blocks[j])

    for i, j in _sort_network(k):
        order(i, j)
    shift = SUBLANES // 2
    while shift >= 1:
        other = [pltpu.roll(b, shift, 0) for b in blocks]
        blocks[:] = [jnp.maximum(blocks[r], other[k - 1 - r]) for r in range(k)]
        dist = k // 2
        while dist >= 1:
            for r in range(k):
                if r & dist == 0:
                    order(r, r + dist)
            dist //= 2
        shift //= 2
    return [b[0:1] for b in blocks]


def _peerq_kernel(x_ref, sh_ref, sc_ref, w_ref, k1_ref, k2_ref, tab_ref, tinv_ref, yt_ref, ys_ref, s1_ref, s2_ref,
                  tc_ref, tab_out_ref, y_ref, *, heads_per_step, half):
    tab_out_ref[...] = (tab_ref[...] * tinv_ref[...]).astype(F8)

    @pl.when(pl.program_id(1) == 0)
    def _():
        y = _modulated_rms(x_ref[...], sc_ref[...], sh_ref[...])
        y_ref[...] = y.astype(BF16)
        yt_ref[...], scale = _quantize_rows(y.T, axis=0)
        ys_ref[...] = jnp.broadcast_to(scale, ys_ref.shape)

    K = PEER_TOPK
    q = _dot(y_ref[...], w_ref[...])
    tm = q.shape[0]
    for hh in range(heads_per_step):
        q1 = q[:, 2 * half * hh:2 * half * hh + half].astype(BF16)
        q2 = q[:, 2 * half * hh + half:2 * half * (hh + 1)].astype(BF16)
        s1 = _dot_nt(k1_ref[...], q1)
        s2 = _dot_nt(k2_ref[...], q2)
        s1_ref[hh] = s1
        s2_ref[hh] = s2
        v1 = _top_values(s1, K)
        v2 = jnp.concatenate(_top_values(s2, K), axis=0)
        cand = [v1[0] + v2]
        for i in range(1, K // 2):
            cand.append(v1[i] + v2[:K // 2])
        cand.append(jnp.concatenate(v1[K // 2:], axis=0) + v2[0:1])
        top = _top_values(jnp.concatenate(cand, axis=0), K)
        m = top[0]
        z = jnp.exp(top[0] - m)
        for r in range(1, K):
            z = z + jnp.exp(top[r] - m)
        tc_ref[hh] = jnp.concatenate([top[K - 1], m + jnp.log(z), jnp.zeros((SUBLANES - 2, tm), F32)], axis=0)


def _peerq(x1, mod, w_q, k1, k2, table, table_inv_scale, seq, tm, n_heads):
    t, d = x1.shape
    nq = w_q.shape[1]
    n_keys, half = k1.shape
    hps = 2
    tn = hps * 2 * half
    n_j = nq // tn
    tab_rows = table.shape[0] // ((t // tm) * n_j)
    assert tab_rows * (t // tm) * n_j == table.shape[0] and tab_rows % (4 * SUBLANES) == 0
    tab_spec = pl.BlockSpec((tab_rows, table.shape[1]), lambda i, j: (i * n_j + j, 0))
    mod_spec = lambda k: pl.BlockSpec((None, None, 1, d), lambda i, j: (k, (i * tm) // seq, 0, 0))
    kern = functools.partial(_peerq_kernel, heads_per_step=hps, half=half)
    return pl.pallas_call(
        kern,
        grid=(t // tm, n_j),
        in_specs=[pl.BlockSpec((tm, d), lambda i, j: (i, 0)),
                  mod_spec(3), mod_spec(4),
                  pl.BlockSpec((d, tn), lambda i, j: (0, j)),
                  pl.BlockSpec(k1.shape, lambda i, j: (0, 0)),
                  pl.BlockSpec(k2.shape, lambda i, j: (0, 0)),
                  tab_spec,
                  pl.BlockSpec(table_inv_scale.shape, lambda i, j: (0, 0))],
        out_specs=[pl.BlockSpec((d, tm), lambda i, j: (0, i)),
                   pl.BlockSpec((SUBLANES, tm), lambda i, j: (0, i)),
                   pl.BlockSpec((hps, n_keys, tm), lambda i, j: (j, 0, i)),
                   pl.BlockSpec((hps, n_keys, tm), lambda i, j: (j, 0, i)),
                   pl.BlockSpec((hps, SUBLANES, tm), lambda i, j: (j, 0, i)),
                   tab_spec],
        out_shape=[jax.ShapeDtypeStruct((d, t), F8),
                   jax.ShapeDtypeStruct((SUBLANES, t), F32),
                   jax.ShapeDtypeStruct((n_heads, n_keys, t), F32),
                   jax.ShapeDtypeStruct((n_heads, n_keys, t), F32),
                   jax.ShapeDtypeStruct((n_heads, SUBLANES, t), F32),
                   jax.ShapeDtypeStruct(table.shape, F8)],
        scratch_shapes=[pltpu.VMEM((tm, d), BF16)],
        compiler_params=_params("arbitrary", "arbitrary"),
        name="peerq",
    )(x1, mod, mod, w_q, k1, k2, table, table_inv_scale)


def _peer_kernel(yt_ref, ys_ref, dn_ref, ds_ref, up_ref, s1_ref, s2_ref, tc_ref, x_ref, g_ref, o_ref,
                 hd_ref, gate_ref, hst_ref, st_ref, *, n_heads, n_keys, te):
    s = pl.program_id(1)
    n = pl.num_programs(1) - 1
    d, tm = yt_ref.shape

    @pl.when((pl.program_id(0) == 0) & (s == 0))
    def _():
        hst_ref[...] = jnp.zeros_like(hst_ref)
        st_ref[...] = jnp.zeros_like(st_ref)

    @pl.when(s == 0)
    def _():
        o_ref[...] = jnp.zeros_like(o_ref)

    per_step = te // n_keys

    @pl.when(s < n)
    def _():
        n_kc = d // MXU_TILE
        per_a = n_kc // per_step
        tw = tm // per_a
        acc = None
        for kc in range(n_kc):
            ks = slice(MXU_TILE * kc, MXU_TILE * (kc + 1))
            part = _dot(dn_ref[:, ks], yt_ref[ks, :])
            acc = part if acc is None else acc + part
            ai, q = divmod(kc, per_a)
            a = s * per_step + ai
            cols = slice(tw * q, tw * (q + 1))
            rows = [tuple(jnp.broadcast_to(r, (SUBLANES, tw)) for r in
                          (s1_ref[h, pl.ds(a, 1), :][:, cols], tc_ref[h, 0:1, cols], tc_ref[h, 1:2, cols]))
                    for h in range(n_heads)]
            for blk in range(n_keys // SUBLANES):
                r0 = SUBLANES * blk
                gate = None
                for h in range(n_heads):
                    s1_a, tau, log_norm = rows[h]
                    cand = s1_a + s2_ref[h, r0:r0 + SUBLANES, cols]
                    term = jnp.where(cand >= tau, jnp.exp(cand - log_norm), 0.0)
                    gate = term if gate is None else gate + term
                gate_ref[n_keys * ai + r0:n_keys * ai + r0 + SUBLANES, cols] = gate
        hd_ref[...] = acc

    cur = s % 2
    th = tm // 2
    tok_scale = _dot_tn(st_ref[1 - cur].astype(BF16), jnp.ones((SUBLANES, LANES), BF16))
    tok_scale = jnp.tile(tok_scale, (1, PEER_OUT_CHUNK // LANES))
    for j in range(d // PEER_OUT_CHUNK):
        cols = slice(PEER_OUT_CHUNK * j, PEER_OUT_CHUNK * (j + 1))
        o_ref[:, cols] += _dot(hst_ref[1 - cur], up_ref[:, cols]) * tok_scale

    pieces = [(slice(n_keys * ai, n_keys * (ai + 1)), slice(th * half, th * (half + 1)), half)
              for ai in range(per_step) for half in range(2)]
    amax = [None, None]
    for rows, tok, half in pieces:
        x = hd_ref[rows, tok] * jnp.tile(ds_ref[rows, :], (1, th // LANES)) * ys_ref[0:1, tok]
        hs = 0.5 * x * (1.0 + lax.erf(x * (2.0 ** -0.5))) * gate_ref[rows, tok]
        hd_ref[rows, tok] = hs
        m = jnp.max(jnp.abs(hs), axis=0, keepdims=True)
        amax[half] = m if amax[half] is None else jnp.maximum(amax[half], m)
    scale = _pow2_scale(jnp.concatenate(amax, axis=1))
    inv = 1.0 / scale
    for rows, tok, half in pieces:
        hst_ref[cur, tok, rows] = (hd_ref[rows, tok] * inv[:, tok]).T.astype(F8)
    st_ref[cur] = jnp.concatenate([scale, jnp.zeros((SUBLANES - 1, tm), F32)], axis=0)

    @pl.when(s == n)
    def _():
        o_ref[...] = x_ref[...] + g_ref[...] * o_ref[...]
        hst_ref[1] = jnp.zeros_like(hst_ref[1])


def _peer(y_t, y_scale, down, down_scale, up, s1, s2, tc, x1, gate_scale, seq, tm, te):
    d, t = y_t.shape
    n = down.shape[0] // te
    n_heads, n_keys, _ = s1.shape
    assert (d // MXU_TILE) % (te // n_keys) == 0 and d % PEER_OUT_CHUNK == 0
    once = dict(pipeline_mode=pl.Buffered(1))
    kern = functools.partial(_peer_kernel, n_heads=n_heads, n_keys=n_keys, te=te)
    return pl.pallas_call(
        kern,
        grid=(t // tm, n + 1),
        in_specs=[pl.BlockSpec((d, tm), lambda i, s: (0, i), **once),
                  pl.BlockSpec((SUBLANES, tm), lambda i, s: (0, i), **once),
                  pl.BlockSpec((te, d), lambda i, s: (jnp.minimum(s, n - 1), 0)),
                  pl.BlockSpec((te, LANES), lambda i, s: (jnp.minimum(s, n - 1), 0)),
                  pl.BlockSpec((te, d), lambda i, s: (jnp.maximum(s - 1, 0), 0)),
                  pl.BlockSpec((n_heads, n_keys, tm), lambda i, s: (0, 0, i), **once),
                  pl.BlockSpec((n_heads, n_keys, tm), lambda i, s: (0, 0, i), **once),
                  pl.BlockSpec((n_heads, SUBLANES, tm), lambda i, s: (0, 0, i), **once),
                  pl.BlockSpec((tm, d), lambda i, s: (i, 0), **once),
                  pl.BlockSpec((None, None, 1, d), lambda i, s: (0, (i * tm) // seq, 0, 0))],
        out_specs=pl.BlockSpec((tm, d), lambda i, s: (i, 0)),
        out_shape=jax.ShapeDtypeStruct((t, d), F32),
        scratch_shapes=[pltpu.VMEM((te, tm), F32), pltpu.VMEM((te, tm), F32), pltpu.VMEM((2, tm, te), F8),
                        pltpu.VMEM((2, SUBLANES, tm), F32)],
        compiler_params=_params("arbitrary", "arbitrary"),
        name="peer",
    )(y_t, y_scale, down, down_scale, up, s1, s2, tc, x1, gate_scale)


def _tile(total, want):
    return want if total % want == 0 else total


def kernel(x, c, w_ada, b_ada, w_in, w_gla_gate_up, b_gla_gate, gla_out_norm, swa_q_norm, swa_k_norm, swa_sinks, swa_out_norm, w_out, w_peer_q, peer_sub_keys_1, peer_sub_keys_2, peer_expert_down, peer_expert_up):
    batch, seq, d = x.shape
    t = batch * seq
    depth = w_ada.shape[0]
    kw = w_gla_gate_up.shape[2]
    dv = gla_out_norm.shape[1]
    dk = dv // 2
    gw = (kw // dk) * dv
    dh = swa_q_norm.shape[1]
    qw = swa_sinks.shape[1] * dh
    kvw = qw // SWA_GROUP
    in_width = w_in.shape[2]
    ga_old = 2 * kw + 2 * gw
    assert in_width == ga_old + GLA_GATE_RANK + qw + 2 * kvw
    assert gw == 2 * kw and qw == gw and d == gw + qw and 2 * dk == LANES and dv == LANES
    assert seq % SWA_BLOCK == 0 and kvw % LANES == 0
    n_keys, half = peer_sub_keys_1.shape[1:]
    n_heads = w_peer_q.shape[2] // (2 * half)
    assert n_keys == LANES and half == LANES and n_heads % 2 == 0

    q_col = ga_old
    n_tail = qw + 2 * kvw
    ga_col = ga_old + n_tail
    tn_in = 512 if ga_old % 512 == 0 and n_tail % 512 == 0 else 256
    assert ga_old % tn_in == 0 and n_tail % tn_in == 0 and qw >= tn_in

    tm_in = _tile(t, 1024)
    ts_gla = _tile(seq, 256)
    tm_out = _tile(seq, 1024)
    tm_peer = _tile(seq, 512)
    te = 1024

    c_pad = jnp.pad(c, ((0, -batch % SUBLANES), (0, 0)))
    x2 = x.reshape(t, d)
    for l in range(depth):
        mod = _ada(c_pad, w_ada[l], b_ada[l][None, :])
        mod = mod[:batch].reshape(batch, N_MOD, d).transpose(1, 0, 2)[:, :, None, :]

        h = _modulate(x2, mod, seq, tm_peer)
        proj = _inproj(h, w_in[l].T, tm_in, tn_in, ga_old, n_tail)

        wup = jnp.pad(w_gla_gate_up[l], ((0, LANES - GLA_GATE_RANK), (0, 0))).astype(BF16)
        o_gla, up_cmax = _gla(proj, wup, b_gla_gate[l][None, :], gla_out_norm[l][None, :], peer_expert_up[l],
                              batch, seq, ts_gla, kw, gw, ga_col, dk, dv)
        up_scale = jnp.maximum(jnp.max(up_cmax, axis=(0, 1)), F8_AMAX_FLOOR)[None, :] * (1.0 / F8_MAX)
        o_swa, down, down_scale = _swa(proj, swa_q_norm[l], swa_k_norm[l], swa_out_norm[l], swa_sinks[l],
                           peer_expert_down[l], batch, seq, q_col, qw, kvw, dh)
        x1 = _outproj(o_gla, o_swa, w_out[l], x2, mod, seq, tm_out, 512)

        y_t, y_scale, s1, s2, tc, up = _peerq(x1, mod, w_peer_q[l].astype(BF16), peer_sub_keys_1[l].astype(BF16),
                                     peer_sub_keys_2[l].astype(BF16), peer_expert_up[l], 1.0 / up_scale,
                                     seq, tm_peer, n_heads)
        x2 = _peer(y_t, y_scale, down, down_scale, up, s1, s2, tc, x1, (mod[5] * up_scale)[None],
                   seq, tm_peer, te)
    return x2.reshape(batch, seq, d)
```

```python
import functools

import jax
import jax.numpy as jnp
from jax import lax
from jax.experimental import pallas as pl
from jax.experimental.pallas import tpu as pltpu

F32 = jnp.float32
BF16 = jnp.bfloat16
F8 = jnp.float8_e4m3fn
F8_MAX = 448.0
F8_AMAX_FLOOR = 2.0 ** -100

NORM_EPS = 1e-6
N_MOD = 6
GLA_GATE_RANK = 16
GLA_GATE_TEMP = 16.0
GLA_CHUNK = 128
SWA_BLOCK = 128
SWA_GROUP = 8
ROPE_THETA = 10000.0
PEER_TOPK = 16

LANES = 128
SUBLANES = 8
MXU_TILE = 256
PEER_OUT_CHUNK = 512
VMEM_LIMIT_BYTES = 60 * 1024 * 1024


def _params(*semantics):
    return pltpu.CompilerParams(dimension_semantics=semantics, vmem_limit_bytes=VMEM_LIMIT_BYTES)


def _dot(a, b):
    return jnp.dot(a, b, preferred_element_type=F32)


def _dot_nt(a, b):
    return lax.dot_general(a, b, (((1,), (1,)), ((), ())), preferred_element_type=F32)


def _dot_tn(a, b):
    return lax.dot_general(a, b, (((0,), (0,)), ((), ())), preferred_element_type=F32)


def _split(x, parts):
    out = []
    for _ in range(parts - 1):
        hi = x.astype(BF16)
        out.append(hi)
        x = x - hi.astype(F32)
    out.append(x.astype(BF16))
    return out


def _dot_split(x, m, parts):
    acc = None
    for p in _split(x, parts):
        t = _dot(p, m)
        acc = t if acc is None else acc + t
    return acc


def _quantize_rows(x, axis):
    amax = jnp.max(jnp.abs(x), axis=axis, keepdims=True)
    scale = jnp.maximum(amax, F8_AMAX_FLOOR) * (1.0 / F8_MAX)
    return (x * (1.0 / scale)).astype(F8), scale


def _pow2_scale(amax):
    bits = lax.bitcast_convert_type(jnp.maximum(amax, F8_AMAX_FLOOR), jnp.int32)
    above = lax.bitcast_convert_type((bits & 0x7F800000) + 0x00800000, F32)
    return above * (1.0 / 256.0)


def _modulated_rms(x, scale, shift):
    ms = jnp.mean(x * x, axis=-1, keepdims=True)
    return x * lax.rsqrt(ms + NORM_EPS) * (1.0 + scale) + shift


def _ada_kernel(c_ref, w_ref, b_ref, o_ref):
    c = c_ref[...]
    act = (c * jax.nn.sigmoid(c)).astype(BF16)
    o_ref[...] = _dot(act, w_ref[...].astype(BF16)) + b_ref[...]


def _ada(c_pad, w, b):
    rows, d = c_pad.shape
    n = w.shape[1]
    tn = 512
    return pl.pallas_call(
        _ada_kernel,
        grid=(n // tn,),
        in_specs=[pl.BlockSpec((rows, d), lambda j: (0, 0)),
                  pl.BlockSpec((d, tn), lambda j: (0, j)),
                  pl.BlockSpec((1, tn), lambda j: (0, j))],
        out_specs=pl.BlockSpec((rows, tn), lambda j: (0, j)),
        out_shape=jax.ShapeDtypeStruct((rows, n), F32),
        compiler_params=_params("arbitrary"),
        name="ada",
    )(c_pad, w, b)


def _modulate_kernel(x_ref, sh_ref, sc_ref, o_ref):
    o_ref[...] = _modulated_rms(x_ref[...], sc_ref[...], sh_ref[...]).astype(BF16)


def _modulate(x2, mod, seq, tm):
    t, d = x2.shape
    mod_spec = lambda k: pl.BlockSpec((None, None, 1, d), lambda i: (k, (i * tm) // seq, 0, 0))
    return pl.pallas_call(
        _modulate_kernel,
        grid=(t // tm,),
        in_specs=[pl.BlockSpec((tm, d), lambda i: (i, 0)), mod_spec(0), mod_spec(1)],
        out_specs=pl.BlockSpec((tm, d), lambda i: (i, 0)),
        out_shape=jax.ShapeDtypeStruct((t, d), BF16),
        compiler_params=_params("arbitrary"),
        name="modulate",
    )(x2, mod, mod)


def _inproj_kernel(h_ref, wt_ref, o_ref, w_ref):
    @pl.when(pl.program_id(1) == 0)
    def _():
        w_ref[...] = wt_ref[...].T.astype(BF16)

    o_ref[...] = _dot(h_ref[...], w_ref[...])


def _inproj(h, w_t, tm, tn, ga_old, n_tail):
    t, d = h.shape
    n_main = ga_old // tn
    n_sw = n_tail // tn

    def src_row(j):
        s = SUBLANES
        return s * jnp.where(j < n_main, j * (tn // s),
                             jnp.where(j < n_main + n_sw, (ga_old + GLA_GATE_RANK) // s + (j - n_main) * (tn // s),
                                       ga_old // s))

    n_tiles = n_main + n_sw + 1
    return pl.pallas_call(
        _inproj_kernel,
        grid=(n_tiles, t // tm),
        in_specs=[pl.BlockSpec((tm, d), lambda j, i: (i, 0)),
                  pl.BlockSpec((pl.Element(tn), pl.Element(d)), lambda j, i: (src_row(j), 0))],
        out_specs=pl.BlockSpec((tm, tn), lambda j, i: (i, j)),
        out_shape=jax.ShapeDtypeStruct((t, n_tiles * tn), F32),
        scratch_shapes=[pltpu.VMEM((d, tn), BF16)],
        compiler_params=_params("arbitrary", "arbitrary"),
        name="inproj",
    )(h, w_t)


def _gla_kernel(q_ref, k_ref, v_ref, g_ref, a_ref, wup_ref, bup_ref, gain_ref, tab_ref, o_ref, cmax_ref, st_ref,
                *, n_pairs, n_chunks, dk, dv):
    cmax_ref[...] = jnp.broadcast_to(jnp.max(jnp.abs(tab_ref[...]), axis=0, keepdims=True), cmax_ref.shape)

    @pl.when(pl.program_id(1) == 0)
    def _():
        st_ref[...] = jnp.zeros_like(st_ref)

    C = GLA_CHUNK
    row = lax.broadcasted_iota(jnp.int32, (C, C), 0)
    col = lax.broadcasted_iota(jnp.int32, (C, C), 1)
    tril = row >= col
    cum_op = tril.astype(BF16)
    causal2 = jnp.concatenate([tril, tril], axis=0)
    head0 = lax.broadcasted_iota(jnp.int32, (C, 2 * dk), 1) < dk
    head0_st = lax.broadcasted_iota(jnp.int32, (dv, 2 * dk), 1) < dk
    scale = dk ** -0.5
    gain = gain_ref[...]

    def stack_heads(x):
        return jnp.concatenate([jnp.where(head0, x, 0.0), jnp.where(head0, 0.0, x)], axis=0).astype(BF16)

    def chunk(c, carry):
        rows = pl.ds(pl.multiple_of(c * C, C), C)
        z = _dot(a_ref[rows, :].astype(BF16), wup_ref[...]) + bup_ref[...]
        log_a = (jnp.minimum(z, 0.0) - jnp.log1p(jnp.exp(-jnp.abs(z)))) * (1.0 / GLA_GATE_TEMP)
        b = _dot_split_left(cum_op, log_a)
        b_last = b[C - 1:C, :]
        mid = 0.5 * b_last
        qt = q_ref[rows, :] * scale * jnp.exp(b - mid)
        kt = k_ref[rows, :] * jnp.exp(mid - b)
        qi = qt * jnp.exp(mid)
        kl = (kt * jnp.exp(b_last - mid)).astype(BF16)
        kt = kt.astype(BF16)
        decay = jnp.exp(b_last)
        pair = lambda x, p: x[:, 2 * dk * p:2 * dk * (p + 1)]
        heads = range(2 * n_pairs)
        st = [st_ref[p] for p in range(n_pairs)]
        attn = [_dot_nt(stack_heads(pair(qt, p)), pair(kt, p)) for p in range(n_pairs)]
        inter = [_dot_nt(stack_heads(pair(qi, p)), st[p].astype(BF16)) for p in range(n_pairs)]
        attn = [jnp.where(causal2, a, 0.0).astype(BF16) for a in attn]
        v = [v_ref[rows, dv * h:dv * (h + 1)].astype(BF16) for h in heads]
        half = lambda x, h: x[h // 2][C * (h % 2):C * (h % 2 + 1)]
        out = [_dot(half(attn, h), v[h]) + half(inter, h) for h in heads]
        upd = [_dot_tn(v[h], pair(kl, h // 2)) for h in heads]
        for h in heads:
            o = out[h]
            on = o * lax.rsqrt(jnp.mean(o * o, axis=-1, keepdims=True) + NORM_EPS) * gain
            gg = g_ref[rows, dv * h:dv * (h + 1)]
            o_ref[rows, dv * h:dv * (h + 1)] = (on * (gg * jax.nn.sigmoid(gg))).astype(BF16)
        for p in range(n_pairs):
            st_ref[p] = st[p] * pair(decay, p) + jnp.where(head0_st, upd[2 * p], upd[2 * p + 1])
        return carry

    lax.fori_loop(0, n_chunks, chunk, 0)


def _dot_split_left(m, x):
    acc = None
    for p in _split(x, 3):
        t = _dot(m, p)
        acc = t if acc is None else acc + t
    return acc


def _gla(proj, wup, bup, gain, table, batch, seq, ts, kw, gw, ga_col, dk, dv):
    t = proj.shape[0]
    nb = seq // ts
    tab_rows = table.shape[0] // (batch * nb)
    assert tab_rows * batch * nb == table.shape[0] and tab_rows % SUBLANES == 0
    rb = lambda b, s: b * nb + s
    kern = functools.partial(_gla_kernel, n_pairs=kw // (2 * dk), n_chunks=ts // GLA_CHUNK, dk=dk, dv=dv)
    return pl.pallas_call(
        kern,
        grid=(batch, nb),
        in_specs=[pl.BlockSpec((ts, kw), lambda b, s: (rb(b, s), 0)),
                  pl.BlockSpec((ts, kw), lambda b, s: (rb(b, s), 1)),
                  pl.BlockSpec((ts, gw), lambda b, s: (rb(b, s), (2 * kw) // gw)),
                  pl.BlockSpec((ts, gw), lambda b, s: (rb(b, s), (2 * kw) // gw + 1)),
                  pl.BlockSpec((ts, LANES), lambda b, s: (rb(b, s), ga_col // LANES)),
                  pl.BlockSpec(wup.shape, lambda b, s: (0, 0)),
                  pl.BlockSpec(bup.shape, lambda b, s: (0, 0)),
                  pl.BlockSpec(gain.shape, lambda b, s: (0, 0)),
                  pl.BlockSpec((tab_rows, table.shape[1]), lambda b, s: (rb(b, s), 0))],
        out_specs=[pl.BlockSpec((ts, gw), lambda b, s: (rb(b, s), 0)),
                   pl.BlockSpec((None, SUBLANES, table.shape[1]), lambda b, s: (rb(b, s), 0, 0))],
        out_shape=[jax.ShapeDtypeStruct((t, gw), BF16),
                   jax.ShapeDtypeStruct((batch * nb, SUBLANES, table.shape[1]), F32)],
        scratch_shapes=[pltpu.VMEM((kw // (2 * dk), dv, 2 * dk), F32)],
        compiler_params=_params("arbitrary", "arbitrary"),
        name="gla",
    )(proj, proj, proj, proj, proj, wup, bup, gain, table)


def _group_rms(x, ones_bd, dh):
    ss = _dot_split(x * x, ones_bd, 2)
    return x * lax.rsqrt(ss * (1.0 / dh) + NORM_EPS)


def _rope(x, cos, sin_signed, dh):
    half = dh // 2
    width = x.shape[1]
    first_half = (lax.broadcasted_iota(jnp.int32, x.shape, 1) & (dh - 1)) < half
    partner = jnp.where(first_half, pltpu.roll(x, width - half, 1), pltpu.roll(x, half, 1))
    return x * cos + partner * sin_signed


def _swa_kernel(q_ref, kc_ref, kp_ref, vc_ref, vp_ref, cc_ref, cp_ref, sc_ref, sp_ref,
                gq_ref, gk_ref, go_ref, sink_ref, ones_ref, sel_ref, tab_ref, o_ref, tab_out_ref, tab_scale_ref,
                *, n_groups, dh):
    W, G = SWA_BLOCK, SWA_GROUP
    n = pl.program_id(1)
    gw = G * dh
    kvw = n_groups * dh
    pairs = G // 2
    tab_out_ref[...], scale = _quantize_rows(tab_ref[...], axis=1)
    tab_scale_ref[...] = jnp.broadcast_to(scale, tab_scale_ref.shape)

    k_all = jnp.concatenate([kp_ref[...], kc_ref[...]], axis=0)
    cos_k = jnp.concatenate([cp_ref[:, :kvw], cc_ref[:, :kvw]], axis=0)
    sin_k = jnp.concatenate([sp_ref[:, :kvw], sc_ref[:, :kvw]], axis=0)
    k_r = _rope(_group_rms(k_all, ones_ref[:kvw, :kvw], dh) * gk_ref[...], cos_k, sin_k, dh)
    k_t = k_r.T.astype(BF16)
    v_all = jnp.concatenate([vp_ref[...], vc_ref[...]], axis=0).astype(BF16)

    qi = lax.broadcasted_iota(jnp.int32, (2 * W, 2 * W), 0) & (W - 1)
    kj = lax.broadcasted_iota(jnp.int32, (2 * W, 2 * W), 1)
    rel = kj - qi - 1
    first_key = jnp.where(n > 0, 0, W)
    valid = (rel >= 0) & (rel < W) & (kj >= first_key)
    head0 = lax.broadcasted_iota(jnp.int32, (W, 2 * dh), 1) < dh
    ones_cols = jnp.ones((2 * W, 2 * dh), BF16)

    for g in range(n_groups):
        q_n = _group_rms(q_ref[:, gw * g:gw * (g + 1)], ones_ref[...], dh) * gq_ref[...]
        q_r = _rope(q_n, cc_ref[...], sc_ref[...], dh)
        k_tg = k_t[dh * g:dh * (g + 1), :]
        k_t2 = jnp.concatenate([k_tg, k_tg], axis=0)
        v3 = jnp.concatenate([_dot(v_all, sel_ref[g]).astype(BF16), ones_cols], axis=1)
        outs = []
        for pp in range(pairs):
            qp = q_r[:, 2 * dh * pp:2 * dh * (pp + 1)]
            q2 = jnp.concatenate([jnp.where(head0, qp, 0.0), jnp.where(head0, 0.0, qp)], axis=0).astype(BF16)
            s = jnp.where(valid, _dot(q2, k_t2) * (dh ** -0.5), -jnp.inf)
            sink = sink_ref[pairs * g + pp]
            m = jnp.maximum(jnp.max(s, axis=-1, keepdims=True), sink)
            r = _dot(jnp.exp(s - m).astype(BF16), v3)
            o2 = r[:, :2 * dh] / (r[:, 2 * dh:] + jnp.exp(sink - m))
            outs.append(jnp.where(head0, o2[:W], o2[W:]))
        o = jnp.concatenate(outs, axis=1)
        o_ref[:, gw * g:gw * (g + 1)] = (_group_rms(o, ones_ref[...], dh) * go_ref[...]).astype(BF16)


def _swa_constants(dh, n_groups, seq):
    G = SWA_GROUP
    gw = G * dh
    half = dh // 2
    i = jnp.arange(gw)
    ones_bd = (i[:, None] // dh == i[None, :] // dh).astype(BF16)
    kvw = n_groups * dh
    j = jnp.arange(2 * dh)
    sel = (jnp.arange(kvw)[None, :, None] == (dh * jnp.arange(n_groups)[:, None, None] + (j % dh)[None, None, :])).astype(BF16)
    inv = ROPE_THETA ** (-jnp.arange(half, dtype=F32) / half)
    ang = jnp.arange(seq, dtype=F32)[:, None] * inv[None, :]
    cos = jnp.tile(jnp.cos(ang), (1, 2 * G))
    sin = jnp.tile(jnp.concatenate([-jnp.sin(ang), jnp.sin(ang)], axis=1), (1, G))
    return ones_bd, sel, cos, sin


def _swa(proj, q_gain, k_gain, o_gain, sinks, table, batch, seq, q_col, qw, kvw, dh):
    t = proj.shape[0]
    tab_rows = table.shape[0] // (batch * (seq // SWA_BLOCK))
    assert tab_rows * batch * (seq // SWA_BLOCK) == table.shape[0] and tab_rows % (4 * SUBLANES) == 0
    tab_spec = pl.BlockSpec((tab_rows, table.shape[1]), lambda b, s: (b * (seq // SWA_BLOCK) + s, 0))
    scale_spec = pl.BlockSpec((tab_rows, LANES), lambda b, s: (b * (seq // SWA_BLOCK) + s, 0))
    W, G = SWA_BLOCK, SWA_GROUP
    gw = G * dh
    n_groups = kvw // dh
    nb = seq // W
    assert dh & (dh - 1) == 0 and 2 * dh == LANES
    ones_bd, sel, cos, sin = _swa_constants(dh, n_groups, seq)
    gq = jnp.tile(q_gain, G)[None, :]
    gk = jnp.tile(k_gain, n_groups)[None, :]
    go = jnp.tile(o_gain, G)[None, :]
    sink_cols = jnp.repeat(sinks.reshape(-1, 2), W, axis=1)[:, :, None]
    cur = lambda b, s: b * nb + s
    prev = lambda b, s: b * nb + jnp.maximum(s - 1, 0)
    k_blk = (q_col + qw) // kvw
    const = lambda a: pl.BlockSpec(a.shape, lambda b, s: (0,) * a.ndim)
    kern = functools.partial(_swa_kernel, n_groups=n_groups, dh=dh)
    return pl.pallas_call(
        kern,
        grid=(batch, nb),
        in_specs=[pl.BlockSpec((W, qw), lambda b, s: (cur(b, s), q_col // qw)),
                  pl.BlockSpec((W, kvw), lambda b, s: (cur(b, s), k_blk)),
                  pl.BlockSpec((W, kvw), lambda b, s: (prev(b, s), k_blk)),
                  pl.BlockSpec((W, kvw), lambda b, s: (cur(b, s), k_blk + 1)),
                  pl.BlockSpec((W, kvw), lambda b, s: (prev(b, s), k_blk + 1)),
                  pl.BlockSpec((W, gw), lambda b, s: (s, 0)),
                  pl.BlockSpec((W, gw), lambda b, s: (jnp.maximum(s - 1, 0), 0)),
                  pl.BlockSpec((W, gw), lambda b, s: (s, 0)),
                  pl.BlockSpec((W, gw), lambda b, s: (jnp.maximum(s - 1, 0), 0)),
                  const(gq), const(gk), const(go), const(sink_cols), const(ones_bd), const(sel), tab_spec],
        out_specs=[pl.BlockSpec((W, qw), lambda b, s: (cur(b, s), 0)), tab_spec, scale_spec],
        out_shape=[jax.ShapeDtypeStruct((t, qw), BF16), jax.ShapeDtypeStruct(table.shape, F8),
                   jax.ShapeDtypeStruct((table.shape[0], LANES), F32)],
        compiler_params=_params("arbitrary", "arbitrary"),
        name="swa",
    )(proj, proj, proj, proj, proj, cos, cos, sin, sin, gq, gk, go, sink_cols, ones_bd, sel, table)


def _outproj_kernel(og_ref, os_ref, wg_ref, ws_ref, x_ref, g_ref, o_ref, wgb_ref, wsb_ref):
    @pl.when(pl.program_id(1) == 0)
    def _():
        wgb_ref[...] = wg_ref[...].astype(BF16)
        wsb_ref[...] = ws_ref[...].astype(BF16)

    mix = _dot(og_ref[...], wgb_ref[...]) + _dot(os_ref[...], wsb_ref[...])
    o_ref[...] = x_ref[...] + g_ref[...] * mix


def _outproj(o_gla, o_swa, w_out, x2, mod, seq, tm, tn):
    t, d = x2.shape
    half = o_gla.shape[1]
    return pl.pallas_call(
        _outproj_kernel,
        grid=(d // tn, t // tm),
        in_specs=[pl.BlockSpec((tm, half), lambda j, i: (i, 0)),
                  pl.BlockSpec((tm, half), lambda j, i: (i, 0)),
                  pl.BlockSpec((half, tn), lambda j, i: (0, j)),
                  pl.BlockSpec((half, tn), lambda j, i: (1, j)),
                  pl.BlockSpec((tm, tn), lambda j, i: (i, j)),
                  pl.BlockSpec((None, None, 1, tn), lambda j, i: (2, (i * tm) // seq, 0, j))],
        out_specs=pl.BlockSpec((tm, tn), lambda j, i: (i, j)),
        out_shape=jax.ShapeDtypeStruct((t, d), F32),
        scratch_shapes=[pltpu.VMEM((half, tn), BF16), pltpu.VMEM((half, tn), BF16)],
        compiler_params=_params("arbitrary", "arbitrary"),
        name="outproj",
    )(o_gla, o_swa, w_out, w_out, x2, mod)


def _sort_network(n):
    pairs = []

    def merge(lo, hi, r):
        step = r * 2
        if step < hi - lo:
            merge(lo, hi, step)
            merge(lo + r, hi, step)
            pairs.extend((i, i + r) for i in range(lo + r, hi - r, step))
        else:
            pairs.append((lo, lo + r))

    def sort(lo, hi):
        if hi - lo >= 1:
            mid = lo + (hi - lo) // 2
            sort(lo, mid)
            sort(mid + 1, hi)
            merge(lo, hi, 1)

    sort(0, n - 1)
    return pairs


def _top_values(s, k):
    blocks = [s[SUBLANES * r:SUBLANES * (r + 1)] for r in range(s.shape[0] // SUBLANES)]
    assert len(blocks) <= k and k & (k - 1) == 0 and s.shape[0] % SUBLANES == 0
    blocks += [jnp.full_like(blocks[0], -jnp.inf)] * (k - len(blocks))

    def order(i, j):
        blocks[i], blocks[j] = jnp.maximum(blocks[i], blocks[j]), jnp.minimum(blocks[i], blocks[j])

    for i, j in _sort_network(k):
        order(i, j)
    shift = SUBLANES // 2
    while shift >= 1:
        other = [pltpu.roll(b, shift, 0) for b in blocks]
        blocks[:] = [jnp.maximum(blocks[r], other[k - 1 - r]) for r in range(k)]
        dist = k // 2
        while dist >= 1:
            for r in range(k):
                if r & dist == 0:
                    order(r, r + dist)
            dist //= 2
        shift //= 2
    return [b[0:1] for b in blocks]


def _peerq_kernel(x_ref, sh_ref, sc_ref, w_ref, k1_ref, k2_ref, tab_ref, tinv_ref, yt_ref, ys_ref, s1_ref, s2_ref,
                  tc_ref, tab_out_ref, y_ref, *, heads_per_step, half):
    tab_out_ref[...] = (tab_ref[...] * tinv_ref[...]).astype(F8)

    @pl.when(pl.program_id(1) == 0)
    def _():
        y = _modulated_rms(x_ref[...], sc_ref[...], sh_ref[...])
        y_ref[...] = y.astype(BF16)
        yt_ref[...], scale = _quantize_rows(y.T, axis=0)
        ys_ref[...] = jnp.broadcast_to(scale, ys_ref.shape)

    K = PEER_TOPK
    q = _dot(y_ref[...], w_ref[...])
    tm = q.shape[0]
    for hh in range(heads_per_step):
        q1 = q[:, 2 * half * hh:2 * half * hh + half].astype(BF16)
        q2 = q[:, 2 * half * hh + half:2 * half * (hh + 1)].astype(BF16)
        s1 = _dot_nt(k1_ref[...], q1)
        s2 = _dot_nt(k2_ref[...], q2)
        s1_ref[hh] = s1
        s2_ref[hh] = s2
        v1 = _top_values(s1, K)
        v2 = jnp.concatenate(_top_values(s2, K), axis=0)
        cand = [v1[0] + v2]
        for i in range(1, K // 2):
            cand.append(v1[i] + v2[:K // 2])
        cand.append(jnp.concatenate(v1[K // 2:], axis=0) + v2[0:1])
        top = _top_values(jnp.concatenate(cand, axis=0), K)
        m = top[0]
        z = jnp.exp(top[0] - m)
        for r in range(1, K):
            z = z + jnp.exp(top[r] - m)
        tc_ref[hh] = jnp.concatenate([top[K - 1], m + jnp.log(z), jnp.zeros((SUBLANES - 2, tm), F32)], axis=0)


def _peerq(x1, mod, w_q, k1, k2, table, table_inv_scale, seq, tm, n_heads):
    t, d = x1.shape
    nq = w_q.shape[1]
    n_keys, half = k1.shape
    hps = 2
    tn = hps * 2 * half
    n_j = nq // tn
    tab_rows = table.shape[0] // ((t // tm) * n_j)
    assert tab_rows * (t // tm) * n_j == table.shape[0] and tab_rows % (4 * SUBLANES) == 0
    tab_spec = pl.BlockSpec((tab_rows, table.shape[1]), lambda i, j: (i * n_j + j, 0))
    mod_spec = lambda k: pl.BlockSpec((None, None, 1, d), lambda i, j: (k, (i * tm) // seq, 0, 0))
    kern = functools.partial(_peerq_kernel, heads_per_step=hps, half=half)
    return pl.pallas_call(
        kern,
        grid=(t // tm, n_j),
        in_specs=[pl.BlockSpec((tm, d), lambda i, j: (i, 0)),
                  mod_spec(3), mod_spec(4),
                  pl.BlockSpec((d, tn), lambda i, j: (0, j)),
                  pl.BlockSpec(k1.shape, lambda i, j: (0, 0)),
                  pl.BlockSpec(k2.shape, lambda i, j: (0, 0)),
                  tab_spec,
                  pl.BlockSpec(table_inv_scale.shape, lambda i, j: (0, 0))],
        out_specs=[pl.BlockSpec((d, tm), lambda i, j: (0, i)),
                   pl.BlockSpec((SUBLANES, tm), lambda i, j: (0, i)),
                   pl.BlockSpec((hps, n_keys, tm), lambda i, j: (j, 0, i)),
                   pl.BlockSpec((hps, n_keys, tm), lambda i, j: (j, 0, i)),
                   pl.BlockSpec((hps, SUBLANES, tm), lambda i, j: (j, 0, i)),
                   tab_spec],
        out_shape=[jax.ShapeDtypeStruct((d, t), F8),
                   jax.ShapeDtypeStruct((SUBLANES, t), F32),
                   jax.ShapeDtypeStruct((n_heads, n_keys, t), F32),
                   jax.ShapeDtypeStruct((n_heads, n_keys, t), F32),
                   jax.ShapeDtypeStruct((n_heads, SUBLANES, t), F32),
                   jax.ShapeDtypeStruct(table.shape, F8)],
        scratch_shapes=[pltpu.VMEM((tm, d), BF16)],
        compiler_params=_params("arbitrary", "arbitrary"),
        name="peerq",
    )(x1, mod, mod, w_q, k1, k2, table, table_inv_scale)


def _peer_kernel(yt_ref, ys_ref, dn_ref, ds_ref, up_ref, s1_ref, s2_ref, tc_ref, x_ref, g_ref, o_ref,
                 hd_ref, gate_ref, hst_ref, st_ref, *, n_heads, n_keys, te):
    s = pl.program_id(1)
    n = pl.num_programs(1) - 1
    d, tm = yt_ref.shape

    @pl.when((pl.program_id(0) == 0) & (s == 0))
    def _():
        hst_ref[...] = jnp.zeros_like(hst_ref)
        st_ref[...] = jnp.zeros_like(st_ref)

    @pl.when(s == 0)
    def _():
        o_ref[...] = jnp.zeros_like(o_ref)

    per_step = te // n_keys

    @pl.when(s < n)
    def _():
        n_kc = d // MXU_TILE
        per_a = n_kc // per_step
        tw = tm // per_a
        acc = None
        for kc in range(n_kc):
            ks = slice(MXU_TILE * kc, MXU_TILE * (kc + 1))
            part = _dot(dn_ref[:, ks], yt_ref[ks, :])
            acc = part if acc is None else acc + part
            ai, q = divmod(kc, per_a)
            a = s * per_step + ai
            cols = slice(tw * q, tw * (q + 1))
            rows = [tuple(jnp.broadcast_to(r, (SUBLANES, tw)) for r in
                          (s1_ref[h, pl.ds(a, 1), :][:, cols], tc_ref[h, 0:1, cols], tc_ref[h, 1:2, cols]))
                    for h in range(n_heads)]
            for blk in range(n_keys // SUBLANES):
                r0 = SUBLANES * blk
                gate = None
                for h in range(n_heads):
                    s1_a, tau, log_norm = rows[h]
                    cand = s1_a + s2_ref[h, r0:r0 + SUBLANES, cols]
                    term = jnp.where(cand >= tau, jnp.exp(cand - log_norm), 0.0)
                    gate = term if gate is None else gate + term
                gate_ref[n_keys * ai + r0:n_keys * ai + r0 + SUBLANES, cols] = gate
        hd_ref[...] = acc

    cur = s % 2
    th = tm // 2
    tok_scale = _dot_tn(st_ref[1 - cur].astype(BF16), jnp.ones((SUBLANES, LANES), BF16))
    tok_scale = jnp.tile(tok_scale, (1, PEER_OUT_CHUNK // LANES))
    for j in range(d // PEER_OUT_CHUNK):
        cols = slice(PEER_OUT_CHUNK * j, PEER_OUT_CHUNK * (j + 1))
        o_ref[:, cols] += _dot(hst_ref[1 - cur], up_ref[:, cols]) * tok_scale

    pieces = [(slice(n_keys * ai, n_keys * (ai + 1)), slice(th * half, th * (half + 1)), half)
              for ai in range(per_step) for half in range(2)]
    amax = [None, None]
    for rows, tok, half in pieces:
        x = hd_ref[rows, tok] * jnp.tile(ds_ref[rows, :], (1, th // LANES)) * ys_ref[0:1, tok]
        hs = 0.5 * x * (1.0 + lax.erf(x * (2.0 ** -0.5))) * gate_ref[rows, tok]
        hd_ref[rows, tok] = hs
        m = jnp.max(jnp.abs(hs), axis=0, keepdims=True)
        amax[half] = m if amax[half] is None else jnp.maximum(amax[half], m)
    scale = _pow2_scale(jnp.concatenate(amax, axis=1))
    inv = 1.0 / scale
    for rows, tok, half in pieces:
        hst_ref[cur, tok, rows] = (hd_ref[rows, tok] * inv[:, tok]).T.astype(F8)
    st_ref[cur] = jnp.concatenate([scale, jnp.zeros((SUBLANES - 1, tm), F32)], axis=0)

    @pl.when(s == n)
    def _():
        o_ref[...] = x_ref[...] + g_ref[...] * o_ref[...]
        hst_ref[1] = jnp.zeros_like(hst_ref[1])


def _peer(y_t, y_scale, down, down_scale, up, s1, s2, tc, x1, gate_scale, seq, tm, te):
    d, t = y_t.shape
    n = down.shape[0] // te
    n_heads, n_keys, _ = s1.shape
    assert (d // MXU_TILE) % (te // n_keys) == 0 and d % PEER_OUT_CHUNK == 0
    once = dict(pipeline_mode=pl.Buffered(1))
    kern = functools.partial(_peer_kernel, n_heads=n_heads, n_keys=n_keys, te=te)
    return pl.pallas_call(
        kern,
        grid=(t // tm, n + 1),
        in_specs=[pl.BlockSpec((d, tm), lambda i, s: (0, i), **once),
                  pl.BlockSpec((SUBLANES, tm), lambda i, s: (0, i), **once),
                  pl.BlockSpec((te, d), lambda i, s: (jnp.minimum(s, n - 1), 0)),
                  pl.BlockSpec((te, LANES), lambda i, s: (jnp.minimum(s, n - 1), 0)),
                  pl.BlockSpec((te, d), lambda i, s: (jnp.maximum(s - 1, 0), 0)),
                  pl.BlockSpec((n_heads, n_keys, tm), lambda i, s: (0, 0, i), **once),
                  pl.BlockSpec((n_heads, n_keys, tm), lambda i, s: (0, 0, i), **once),
                  pl.BlockSpec((n_heads, SUBLANES, tm), lambda i, s: (0, 0, i), **once),
                  pl.BlockSpec((tm, d), lambda i, s: (i, 0), **once),
                  pl.BlockSpec((None, None, 1, d), lambda i, s: (0, (i * tm) // seq, 0, 0))],
        out_specs=pl.BlockSpec((tm, d), lambda i, s: (i, 0)),
        out_shape=jax.ShapeDtypeStruct((t, d), F32),
        scratch_shapes=[pltpu.VMEM((te, tm), F32), pltpu.VMEM((te, tm), F32), pltpu.VMEM((2, tm, te), F8),
                        pltpu.VMEM((2, SUBLANES, tm), F32)],
        compiler_params=_params("arbitrary", "arbitrary"),
        name="peer",
    )(y_t, y_scale, down, down_scale, up, s1, s2, tc, x1, gate_scale)


def _tile(total, want):
    return want if total % want == 0 else total


def kernel(x, c, w_ada, b_ada, w_in, w_gla_gate_up, b_gla_gate, gla_out_norm, swa_q_norm, swa_k_norm, swa_sinks, swa_out_norm, w_out, w_peer_q, peer_sub_keys_1, peer_sub_keys_2, peer_expert_down, peer_expert_up):
    batch, seq, d = x.shape
    t = batch * seq
    depth = w_ada.shape[0]
    kw = w_gla_gate_up.shape[2]
    dv = gla_out_norm.shape[1]
    dk = dv // 2
    gw = (kw // dk) * dv
    dh = swa_q_norm.shape[1]
    qw = swa_sinks.shape[1] * dh
    kvw = qw // SWA_GROUP
    in_width = w_in.shape[2]
    ga_old = 2 * kw + 2 * gw
    assert in_width == ga_old + GLA_GATE_RANK + qw + 2 * kvw
    assert gw == 2 * kw and qw == gw and d == gw + qw and 2 * dk == LANES and dv == LANES
    assert seq % SWA_BLOCK == 0 and kvw % LANES == 0
    n_keys, half = peer_sub_keys_1.shape[1:]
    n_heads = w_peer_q.shape[2] // (2 * half)
    assert n_keys == LANES and half == LANES and n_heads % 2 == 0

    q_col = ga_old
    n_tail = qw + 2 * kvw
    ga_col = ga_old + n_tail
    tn_in = 512 if ga_old % 512 == 0 and n_tail % 512 == 0 else 256
    assert ga_old % tn_in == 0 and n_tail % tn_in == 0 and qw >= tn_in

    tm_in = _tile(t, 1024)
    ts_gla = _tile(seq, 256)
    tm_out = _tile(seq, 1024)
    tm_peer = _tile(seq, 512)
    te = 1024

    c_pad = jnp.pad(c, ((0, -batch % SUBLANES), (0, 0)))
    x2 = x.reshape(t, d)
    for l in range(depth):
        mod = _ada(c_pad, w_ada[l], b_ada[l][None, :])
        mod = mod[:batch].reshape(batch, N_MOD, d).transpose(1, 0, 2)[:, :, None, :]

        h = _modulate(x2, mod, seq, tm_peer)
        proj = _inproj(h, w_in[l].T, tm_in, tn_in, ga_old, n_tail)

        wup = jnp.pad(w_gla_gate_up[l], ((0, LANES - GLA_GATE_RANK), (0, 0))).astype(BF16)
        o_gla, up_cmax = _gla(proj, wup, b_gla_gate[l][None, :], gla_out_norm[l][None, :], peer_expert_up[l],
                              batch, seq, ts_gla, kw, gw, ga_col, dk, dv)
        up_scale = jnp.maximum(jnp.max(up_cmax, axis=(0, 1)), F8_AMAX_FLOOR)[None, :] * (1.0 / F8_MAX)
        o_swa, down, down_scale = _swa(proj, swa_q_norm[l], swa_k_norm[l], swa_out_norm[l], swa_sinks[l],
                           peer_expert_down[l], batch, seq, q_col, qw, kvw, dh)
        x1 = _outproj(o_gla, o_swa, w_out[l], x2, mod, seq, tm_out, 512)

        y_t, y_scale, s1, s2, tc, up = _peerq(x1, mod, w_peer_q[l].astype(BF16), peer_sub_keys_1[l].astype(BF16),
                                     peer_sub_keys_2[l].astype(BF16), peer_expert_up[l], 1.0 / up_scale,
                                     seq, tm_peer, n_heads)
        x2 = _peer(y_t, y_scale, down, down_scale, up, s1, s2, tc, x1, (mod[5] * up_scale)[None],
                   seq, tm_peer, te)
    return x2.reshape(batch, seq, d)
```
